```python
import math
import jax
import jax.numpy as jnp
from jax import lax
import numpy as np

D_MODEL = 1024
BATCH = 2
SEQ = 8192
DEPTH = 2
DEC_BATCH = 16
DEC_SEQ = 4096
PAST_LEN = 128

BRANCH_W = 512
N_BRANCH = 4
RWKV_HEADS = 8
RWKV_HD = 64
W_LORA = 64
A_LORA = 64
G_LORA = 128
DIFF_HEADS = 4
DIFF_HD = 64
DIFF_VD = 128
ROPE_DIM = DIFF_HD // 4
ROPE_THETA = 500000.0
Q_BLOCK = 128
FNET_GROUPS = 4
FNET_GD = BRANCH_W // FNET_GROUPS
POOL_GROUPS = 4
POOL_GD = BRANCH_W // POOL_GROUPS
POOL_WINDOWS = (2, 4, 8, 16)
D_FF = 2816
LN_EPS = 1e-5
LNX_EPS = 64e-5
RMS_EPS = 1e-5

RWKV_COLS = 3 * BRANCH_W + 2 * W_LORA + 2 * A_LORA + G_LORA
DIFF_COLS = 2 * DIFF_HEADS * 2 * DIFF_HD + DIFF_HEADS * DIFF_VD
FNET_COLS = BRANCH_W
POOL_COLS = BRANCH_W
GATE_COLS = N_BRANCH * D_MODEL
OFF_DIFF = RWKV_COLS
OFF_FNET = OFF_DIFF + DIFF_COLS
OFF_POOL = OFF_FNET + FNET_COLS
OFF_GATE = OFF_POOL + POOL_COLS
N_IN = OFF_GATE + GATE_COLS

kernel_name = 'hybrid_rwkv7_diffattn_fnet_pool_encoder'


def layer_norm(x, g=None, b=None, eps=LN_EPS):
    xf = x.astype(jnp.float32)
    mu = jnp.mean(xf, axis=-1, keepdims=True)
    var = jnp.mean(jnp.square(xf - mu), axis=-1, keepdims=True)
    y = (xf - mu) * lax.rsqrt(var + eps)
    if g is not None:
        y = y * g.astype(jnp.float32) + b.astype(jnp.float32)
    return y.astype(x.dtype)


def swiglu(h, w_in, w_out):
    u = h @ w_in
    a, b = jnp.split(u, 2, axis=-1)
    return (jax.nn.silu(a) * b) @ w_out


def centred_shift(t):
    tp = jnp.pad(t, ((0, 0), (1, 1), (0, 0)))
    return 0.5 * (tp[:, :-2] + tp[:, 2:])


def wkv7_scan(r, w, k, v, kk, a, reverse):
    B, S, H, N = r.shape
    tm = lambda t: jnp.swapaxes(t, 0, 1)

    def step(state, inp):
        r_t, w_t, k_t, v_t, kk_t, a_t = inp
        sa = jnp.einsum('bhvk,bhk->bhv', state, -kk_t)
        state = (state * w_t[:, :, None, :]
                 + sa[..., None] * (kk_t * a_t)[:, :, None, :]
                 + v_t[..., None] * k_t[:, :, None, :])
        y = jnp.einsum('bhvk,bhk->bhv', state, r_t)
        return state, y

    s0 = jnp.zeros((B, H, N, N), jnp.float32)
    _, ys = lax.scan(step, s0, (tm(r), tm(w), tm(k), tm(v), tm(kk), tm(a)), reverse=reverse)
    return tm(ys)


def rwkv7_mixer(p, mu, w0, w2, a0, a2, g2, k_k, k_a, r_k, lnx_g, lnx_b):
    dt = p.dtype
    B, S, _ = p.shape
    p = p.astype(jnp.float32)
    p = p + (centred_shift(p) - p) * mu
    sizes = (BRANCH_W, BRANCH_W, BRANCH_W, W_LORA, W_LORA, A_LORA, A_LORA)
    r, k, v, w1f, w1b, a1f, a1b, g1 = jnp.split(p, list(np.cumsum(sizes)), axis=-1)
    heads = lambda t: t.reshape(B, S, RWKV_HEADS, RWKV_HD)
    kk = heads(k * k_k)
    kk = kk / jnp.maximum(jnp.sqrt(jnp.sum(kk * kk, axis=-1, keepdims=True)), 1e-12)
    y = 0.0
    k_sum = 0.0
    for d, (w1, a1) in enumerate(((w1f, a1f), (w1b, a1b))):
        w_log = -jax.nn.softplus(-(w0[d] + jnp.tanh(w1) @ w2[d])) - 0.5
        decay = jnp.exp(-jnp.exp(w_log))
        a = jax.nn.sigmoid(a0[d] + a1 @ a2[d])
        k_d = k * (1.0 + (a - 1.0) * k_a)
        y = y + wkv7_scan(heads(r), heads(decay), heads(k_d), heads(v), kk, heads(a), reverse=(d == 1))
        k_sum = k_sum + k_d
    ym = jnp.mean(y, axis=-1, keepdims=True)
    yv = jnp.mean(jnp.square(y - ym), axis=-1, keepdims=True)
    y = ((y - ym) * lax.rsqrt(yv + LNX_EPS)).reshape(B, S, BRANCH_W) * lnx_g + lnx_b
    bonus = jnp.sum(heads(r * (0.5 * k_sum) * r_k), axis=-1, keepdims=True) * heads(v)
    y = y + bonus.reshape(B, S, BRANCH_W)
    g = jax.nn.sigmoid(g1) @ g2
    return (y * g).astype(dt)


def rope_tables(S):
    inv = ROPE_THETA ** (-jnp.arange(0, ROPE_DIM, 2, dtype=jnp.float32) / ROPE_DIM)
    ang = jnp.arange(S, dtype=jnp.float32)[:, None] * inv[None, :]
    return jnp.cos(ang), jnp.sin(ang)


def apply_partial_rope(x, cos, sin):
    half = ROPE_DIM // 2
    c = cos[None, :, None, None, :]
    s = sin[None, :, None, None, :]
    xr = x[..., :ROPE_DIM].astype(jnp.float32)
    x1, x2 = xr[..., :half], xr[..., half:]
    rot = jnp.concatenate([x1 * c - x2 * s, x1 * s + x2 * c], axis=-1).astype(x.dtype)
    return jnp.concatenate([rot, x[..., ROPE_DIM:]], axis=-1)


def diff_attention(p, lam, norm_g, lam_init):
    B, S, _ = p.shape
    H = DIFF_HEADS
    nq = H * 2 * DIFF_HD
    q = p[..., :nq].reshape(B, S, H, 2, DIFF_HD)
    k = p[..., nq:2 * nq].reshape(B, S, H, 2, DIFF_HD)
    v = p[..., 2 * nq:].reshape(B, S, H, DIFF_VD)
    cos, sin = rope_tables(S)
    q = apply_partial_rope(q, cos, sin)
    k = apply_partial_rope(k, cos, sin)
    q = q.transpose(0, 2, 3, 1, 4) * (DIFF_HD ** -0.5)
    k = k.transpose(0, 2, 3, 1, 4)
    v = v.transpose(0, 2, 1, 3)
    lf = lam.astype(jnp.float32)
    lam_full = jnp.exp(jnp.sum(lf[0] * lf[1])) - jnp.exp(jnp.sum(lf[2] * lf[3])) + lam_init
    nb = S // Q_BLOCK
    qb = q.reshape(B, H, 2, nb, Q_BLOCK, DIFF_HD).transpose(3, 0, 1, 2, 4, 5)

    def attend(qblk):
        s = jnp.einsum('bhjqd,bhjkd->bhjqk', qblk, k, preferred_element_type=jnp.float32)
        pr = jax.nn.softmax(s, axis=-1)
        a = pr[:, :, 0] - lam_full * pr[:, :, 1]
        return jnp.einsum('bhqk,bhkv->bhqv', a.astype(v.dtype), v)

    o = lax.map(attend, qb)
    o = o.transpose(1, 0, 3, 2, 4).reshape(B, S, H, DIFF_VD).astype(jnp.float32)
    o = o * lax.rsqrt(jnp.mean(o * o, axis=-1, keepdims=True) + RMS_EPS) * norm_g * (1.0 - lam_init)
    return o.reshape(B, S, H * DIFF_VD).astype(p.dtype)


def fourier_mixer(p):
    B, S, _ = p.shape
    xg = p.reshape(B, S, FNET_GROUPS, FNET_GD).astype(jnp.float32)
    f = jnp.fft.fftn(xg, axes=(1, 3), norm='ortho').real
    return f.reshape(B, S, FNET_COLS).astype(p.dtype)


def pool_mixer(p, scale):
    B, S, _ = p.shape
    xg = p.reshape(B, S, POOL_GROUPS, POOL_GD).astype(jnp.float32)
    cs = jnp.concatenate([jnp.zeros((B, 1, POOL_GROUPS, POOL_GD), jnp.float32), jnp.cumsum(xg, axis=1)], axis=1)
    half = jnp.array([w // 2 for w in POOL_WINDOWS], jnp.int32)[None, :]
    t = jnp.arange(S, dtype=jnp.int32)[:, None]
    lo = jnp.clip(t - half, 0, S)
    hi = jnp.clip(t + half, 0, S)
    gi = jnp.arange(POOL_GROUPS, dtype=jnp.int32)[None, :]
    win_sum = cs[:, hi, gi, :] - cs[:, lo, gi, :]
    count = (hi - lo).astype(jnp.float32)[None, :, :, None]
    pooled = win_sum / count - xg
    return (pooled.reshape(B, S, POOL_COLS) * scale).astype(p.dtype)


def encoder_layer(x, c, lam_init, ada_w, ada_b, ln_g, ln_b, ffa_w_in, ffa_w_out, ffb_w_in, ffb_w_out,
                  w_in, rwkv_mu, rwkv_w0, rwkv_w2, rwkv_a0, rwkv_a2, rwkv_g2, rwkv_kk, rwkv_ka, rwkv_rk,
                  rwkv_lnx_g, rwkv_lnx_b, diff_lam, diff_norm_g, pool_scale, w_branch, w_out):
    B, S, D = x.shape
    alpha = (2.0 * DEPTH) ** 0.25
    mod = (jax.nn.silu(c) @ ada_w + ada_b).reshape(B, 9, 1, D)

    def modulate(t, j):
        return layer_norm(t) * (1.0 + mod[:, 3 * j + 1]) + mod[:, 3 * j]

    def post(t, update, j):
        return layer_norm(alpha * t + (1.0 + mod[:, 3 * j + 2]) * update, ln_g[j], ln_b[j])

    x = post(x, 0.5 * swiglu(modulate(x, 0), ffa_w_in, ffa_w_out), 0)
    h = modulate(x, 1)
    p = h @ w_in
    y_rwkv = rwkv7_mixer(p[..., :OFF_DIFF], rwkv_mu, rwkv_w0, rwkv_w2, rwkv_a0, rwkv_a2, rwkv_g2,
                         rwkv_kk, rwkv_ka, rwkv_rk, rwkv_lnx_g, rwkv_lnx_b)
    y_diff = diff_attention(p[..., OFF_DIFF:OFF_FNET], diff_lam, diff_norm_g, lam_init)
    y_fnet = fourier_mixer(p[..., OFF_FNET:OFF_POOL])
    y_pool = pool_mixer(p[..., OFF_POOL:OFF_GATE], pool_scale)
    gates = jax.nn.sigmoid(p[..., OFF_GATE:].astype(jnp.float32)).astype(x.dtype).reshape(B, S, N_BRANCH, D)
    branches = jnp.stack([y_rwkv, y_diff, y_fnet, y_pool], axis=2)
    proj = jnp.einsum('bsnc,ncd->bsnd', branches, w_branch)
    merged = jnp.sum(gates * proj, axis=2)
    x = post(x, merged @ w_out, 1)
    x = post(x, 0.5 * swiglu(modulate(x, 2), ffb_w_in, ffb_w_out), 2)
    return x


def setup_inputs(seed: int = 0) -> dict:
    key = jax.random.key(seed)
    ks = jax.random.split(key, 32)
    f32 = jnp.float32
    nrm = lambda k, shape, s: jax.random.normal(k, shape, f32) * s
    beta = (8.0 * DEPTH) ** -0.25
    D = D_MODEL
    return {
        'x_prompt': nrm(ks[0], (BATCH, SEQ, D), 1.0),
        'x_sample': nrm(ks[1], (DEC_BATCH, DEC_SEQ, D), 1.0),
        'c_prompt': nrm(ks[2], (BATCH, D), 1.0),
        'c_sample': nrm(ks[3], (DEC_BATCH, D), 1.0),
        'ada_w': nrm(ks[4], (DEPTH, D, 9 * D), 0.1 * D ** -0.5),
        'ada_b': nrm(ks[5], (DEPTH, 9 * D), 0.01),
        'ln_g': 1.0 + nrm(ks[6], (DEPTH, 3, D), 0.05),
        'ln_b': nrm(ks[7], (DEPTH, 3, D), 0.02),
        'ffa_w_in': nrm(ks[8], (DEPTH, D, 2 * D_FF), D ** -0.5),
        'ffa_w_out': nrm(ks[9], (DEPTH, D_FF, D), beta * D_FF ** -0.5),
        'ffb_w_in': nrm(ks[10], (DEPTH, D, 2 * D_FF), D ** -0.5),
        'ffb_w_out': nrm(ks[11], (DEPTH, D_FF, D), beta * D_FF ** -0.5),
        'w_in': nrm(ks[12], (DEPTH, D, N_IN), D ** -0.5),
        'rwkv_mu': jax.random.uniform(ks[13], (DEPTH, RWKV_COLS), f32),
        'rwkv_w0': nrm(ks[14], (DEPTH, 2, BRANCH_W), 0.5),
        'rwkv_w2': nrm(ks[15], (DEPTH, 2, W_LORA, BRANCH_W), 0.5 * W_LORA ** -0.5),
        'rwkv_a0': nrm(ks[16], (DEPTH, 2, BRANCH_W), 0.3),
        'rwkv_a2': nrm(ks[17], (DEPTH, 2, A_LORA, BRANCH_W), 0.5 * A_LORA ** -0.5),
        'rwkv_g2': nrm(ks[18], (DEPTH, G_LORA, BRANCH_W), G_LORA ** -0.5),
        'rwkv_kk': 0.85 + nrm(ks[19], (DEPTH, BRANCH_W), 0.05),
        'rwkv_ka': 1.0 + nrm(ks[20], (DEPTH, BRANCH_W), 0.05),
        'rwkv_rk': nrm(ks[21], (DEPTH, BRANCH_W), 0.3),
        'rwkv_lnx_g': 1.0 + nrm(ks[22], (DEPTH, BRANCH_W), 0.05),
        'rwkv_lnx_b': nrm(ks[23], (DEPTH, BRANCH_W), 0.02),
        'diff_lam': nrm(ks[24], (DEPTH, 4, DIFF_HD), 0.1),
        'diff_norm_g': 1.0 + nrm(ks[25], (DEPTH, DIFF_VD), 0.05),
        'pool_scale': 1.0 + nrm(ks[26], (DEPTH, POOL_COLS), 0.1),
        'w_branch': nrm(ks[27], (DEPTH, N_BRANCH, BRANCH_W, D), BRANCH_W ** -0.5),
        'w_out': nrm(ks[28], (DEPTH, D, D), beta * D ** -0.5),
    }


def reference(x_prompt, x_sample, c_prompt, c_sample, ada_w, ada_b, ln_g, ln_b, ffa_w_in, ffa_w_out,
              ffb_w_in, ffb_w_out, w_in, rwkv_mu, rwkv_w0, rwkv_w2, rwkv_a0, rwkv_a2, rwkv_g2, rwkv_kk,
              rwkv_ka, rwkv_rk, rwkv_lnx_g, rwkv_lnx_b, diff_lam, diff_norm_g, pool_scale, w_branch, w_out):
    def run(x, c):
        for l in range(DEPTH):
            lam_init = 0.8 - 0.6 * math.exp(-0.3 * l)
            x = encoder_layer(x, c, lam_init, ada_w[l], ada_b[l], ln_g[l], ln_b[l], ffa_w_in[l], ffa_w_out[l],
                              ffb_w_in[l], ffb_w_out[l], w_in[l], rwkv_mu[l], rwkv_w0[l], rwkv_w2[l],
                              rwkv_a0[l], rwkv_a2[l], rwkv_g2[l], rwkv_kk[l], rwkv_ka[l], rwkv_rk[l],
                              rwkv_lnx_g[l], rwkv_lnx_b[l], diff_lam[l], diff_norm_g[l], pool_scale[l],
                              w_branch[l], w_out[l])
        return x

    y_prompt = run(x_prompt, c_prompt)
    y_sample = run(x_sample, c_sample)
    return (y_prompt, y_sample)
```

```python
import functools
import math

import numpy as np
import jax
import jax.numpy as jnp
from jax import lax
from jax.experimental import pallas as pl
from jax.experimental.pallas import tpu as pltpu

D_MODEL = 1024
DEPTH = 2
BRANCH_W = 512
N_BRANCH = 4
RWKV_HEADS = 8
RWKV_HD = 64
W_LORA = 64
A_LORA = 64
G_LORA = 128
DIFF_HEADS = 4
DIFF_HD = 64
DIFF_VD = 128
ROPE_DIM = DIFF_HD // 4
ROPE_THETA = 500000.0
FNET_GROUPS = 4
FNET_GD = BRANCH_W // FNET_GROUPS
POOL_GROUPS = 4
POOL_GD = BRANCH_W // POOL_GROUPS
POOL_WINDOWS = (2, 4, 8, 16)
D_FF = 2816
LN_EPS = 1e-5
LNX_EPS = 64e-5
RMS_EPS = 1e-5

RWKV_COLS = 3 * BRANCH_W + 2 * W_LORA + 2 * A_LORA + G_LORA
DIFF_COLS = 2 * DIFF_HEADS * 2 * DIFF_HD + DIFF_HEADS * DIFF_VD
OFF_DIFF = RWKV_COLS
OFF_FNET = OFF_DIFF + DIFF_COLS
OFF_POOL = OFF_FNET + BRANCH_W
OFF_GATE = OFF_POOL + BRANCH_W
N_IN = OFF_GATE + N_BRANCH * D_MODEL
OFF_W1 = 3 * BRANCH_W
OFF_A1 = OFF_W1 + 2 * W_LORA
OFF_G1 = OFF_A1 + 2 * A_LORA

V7X_SUBLANES = 8
V7X_LANES = 128
V7X_VMEM_LIMIT_BYTES = 56 * 1024 * 1024

TOKEN_TILE = 512
FF_CHUNK = 256
SCAN_CHUNK = 64
HEAD_PACK = 4
PACK_W = HEAD_PACK * RWKV_HD
ATT_Q_TILE = 256
ATT_K_TILE = 512
FNET_N2 = 128
FNET_F1_TILE = 8
FNET_COL_TILE = 8

BF16 = jnp.bfloat16
F32 = jnp.float32


def _dot(a, b):
    return jnp.dot(a, b, preferred_element_type=F32)


def _dot_nt(a, b):
    return lax.dot_general(a, b, (((1,), (1,)), ((), ())), preferred_element_type=F32)


def _dot_tn(a, b):
    return lax.dot_general(a, b, (((0,), (0,)), ((), ())), preferred_element_type=F32)


def _split_bf16(x):
    hi = x.astype(BF16)
    lo = (x - hi.astype(F32)).astype(BF16)
    return hi, lo


def _layer_norm(x, eps=LN_EPS):
    mu = jnp.mean(x, axis=-1, keepdims=True)
    xc = x - mu
    var = jnp.mean(xc * xc, axis=-1, keepdims=True)
    return xc * lax.rsqrt(var + eps)


def _params(*semantics):
    return pltpu.CompilerParams(dimension_semantics=semantics, vmem_limit_bytes=V7X_VMEM_LIMIT_BYTES)


def _resident(shape):
    nd = len(shape)
    return pl.BlockSpec(shape, lambda *_: (0,) * nd, pipeline_mode=pl.Buffered(1))


def _mod_kernel(c_ref, w_ref, b_ref, o_ref):
    c = c_ref[...]
    a = c * jax.nn.sigmoid(c)
    a_hi, a_lo = _split_bf16(a)
    w_hi, w_lo = _split_bf16(w_ref[...])
    o_ref[...] = _dot(a_hi, w_hi) + _dot(a_lo, w_hi) + _dot(a_hi, w_lo) + b_ref[...]


def _modulation(c_pad, ada_w, ada_b):
    rows = c_pad.shape[0]
    n = ada_w.shape[1]
    nb = 9 * V7X_LANES
    return pl.pallas_call(
        _mod_kernel,
        grid=(n // nb,),
        in_specs=[pl.BlockSpec((rows, D_MODEL), lambda i: (0, 0)),
                  pl.BlockSpec((D_MODEL, nb), lambda i: (0, i)),
                  pl.BlockSpec((1, nb), lambda i: (0, i))],
        out_specs=pl.BlockSpec((rows, nb), lambda i: (0, i)),
        out_shape=jax.ShapeDtypeStruct((rows, n), F32),
        compiler_params=_params("arbitrary"),
        name="adaln_mod",
    )(c_pad, ada_w, ada_b.reshape(1, n))


def _ffn_kernel(x_ref, mod_ref, g_ref, b_ref, win_ref, wout_ref, o_ref, *, sub, alpha):
    x = x_ref[...]
    mod = mod_ref[0]
    shift, scale, gate = mod[3 * sub:3 * sub + 1], mod[3 * sub + 1:3 * sub + 2], mod[3 * sub + 2:3 * sub + 3]
    h = (_layer_norm(x) * (1.0 + scale) + shift).astype(BF16)
    acc = jnp.zeros(x.shape, F32)
    for c in range(D_FF // FF_CHUNK):
        lo = c * FF_CHUNK
        ua = _dot(h, win_ref[:, lo:lo + FF_CHUNK])
        ub = _dot(h, win_ref[:, D_FF + lo:D_FF + lo + FF_CHUNK])
        act = (ua * jax.nn.sigmoid(ua) * ub).astype(BF16)
        acc = acc + _dot(act, wout_ref[lo:lo + FF_CHUNK, :])
    y = alpha * x + (1.0 + gate) * (0.5 * acc)
    o_ref[...] = _layer_norm(y) * g_ref[...] + b_ref[...]


def _ffn_sublayer(x2, mod, ln_g, ln_b, w_in, w_out, *, sub, seq, alpha):
    t = x2.shape[0]
    tm = min(TOKEN_TILE, seq)
    tps = seq // tm
    return pl.pallas_call(
        functools.partial(_ffn_kernel, sub=sub, alpha=alpha),
        grid=(t // tm,),
        in_specs=[pl.BlockSpec((tm, D_MODEL), lambda i: (i, 0)),
                  pl.BlockSpec((1, 9, D_MODEL), lambda i: (i // tps, 0, 0)),
                  _resident((1, D_MODEL)), _resident((1, D_MODEL)),
                  _resident((D_MODEL, 2 * D_FF)), _resident((D_FF, D_MODEL))],
        out_specs=pl.BlockSpec((tm, D_MODEL), lambda i: (i, 0)),
        out_shape=jax.ShapeDtypeStruct((t, D_MODEL), F32),
        compiler_params=_params("parallel"),
        name=f"ffn_sublayer{sub}",
    )(x2, mod, ln_g, ln_b, w_in, w_out)


def _proj_kernel(x_ref, mod_ref, w_ref, rc_ref, rs1_ref, rs2_ref, fmat_ref,
                 o_rwkv, o_diff, o_fnet, o_pool, o_gate):
    x = x_ref[...]
    mod = mod_ref[0]
    h = (_layer_norm(x) * (1.0 + mod[4:5]) + mod[3:4]).astype(BF16)

    for lo in range(0, RWKV_COLS, 640):
        o_rwkv[:, lo:lo + 640] = _dot(h, w_ref[:, lo:lo + 640])

    rc = jnp.concatenate([rc_ref[...]] * 4, axis=1)
    rs1 = jnp.concatenate([rs1_ref[...]] * 4, axis=1)
    rs2 = jnp.concatenate([rs2_ref[...]] * 4, axis=1)
    for blk, scl in ((0, DIFF_HD ** -0.5), (1, 1.0)):
        lo = OFF_DIFF + blk * 512
        u = _dot(h, w_ref[:, lo:lo + 512])
        rot = u * rc + pltpu.roll(u, 512 - ROPE_DIM // 2, 1) * rs1 + pltpu.roll(u, ROPE_DIM // 2, 1) * rs2
        o_diff[:, blk * 512:(blk + 1) * 512] = (rot * scl).astype(BF16)
    lo = OFF_DIFF + 1024
    o_diff[:, 1024:1536] = _dot(h, w_ref[:, lo:lo + 512]).astype(BF16)

    pf = _dot(h, w_ref[:, OFF_FNET:OFF_FNET + BRANCH_W]).astype(BF16)
    o_fnet[...] = _dot(pf, fmat_ref[...]).astype(BF16)

    o_pool[...] = _dot(h, w_ref[:, OFF_POOL:OFF_POOL + BRANCH_W])

    for c in range(N_BRANCH * D_MODEL // 512):
        lo = OFF_GATE + c * 512
        o_gate[:, c * 512:(c + 1) * 512] = jax.nn.sigmoid(_dot(h, w_ref[:, lo:lo + 512])).astype(BF16)


def _mixer_projection(x2, mod, w_in, rope, fmat, *, seq):
    t = x2.shape[0]
    tm = min(TOKEN_TILE, seq)
    tps = seq // tm
    row = lambda i: (i, 0)
    rope_spec = pl.BlockSpec((tm, V7X_LANES), lambda i: (i % tps, 0))
    return pl.pallas_call(
        _proj_kernel,
        grid=(t // tm,),
        in_specs=[pl.BlockSpec((tm, D_MODEL), row),
                  pl.BlockSpec((1, 9, D_MODEL), lambda i: (i // tps, 0, 0)),
                  _resident((D_MODEL, N_IN)),
                  rope_spec, rope_spec, rope_spec,
                  _resident((BRANCH_W, 2 * BRANCH_W))],
        out_specs=[pl.BlockSpec((tm, RWKV_COLS), row),
                   pl.BlockSpec((tm, DIFF_COLS), row),
                   pl.BlockSpec((tm, 2 * BRANCH_W), row),
                   pl.BlockSpec((tm, BRANCH_W), row),
                   pl.BlockSpec((tm, N_BRANCH * D_MODEL), row)],
        out_shape=[jax.ShapeDtypeStruct((t, RWKV_COLS), F32),
                   jax.ShapeDtypeStruct((t, DIFF_COLS), BF16),
                   jax.ShapeDtypeStruct((t, 2 * BRANCH_W), BF16),
                   jax.ShapeDtypeStruct((t, BRANCH_W), F32),
                   jax.ShapeDtypeStruct((t, N_BRANCH * D_MODEL), BF16)],
        compiler_params=_params("parallel"),
        name="mixer_projection",
    )(x2, mod, w_in, rope[0], rope[1], rope[2], fmat)


def _token_shift(p, prev8, next8, first, last, mu):
    n = p.shape[0]
    prev_row = prev8[V7X_SUBLANES - 1:V7X_SUBLANES, :] * (1.0 - first.astype(F32))
    next_row = next8[0:1, :] * (1.0 - last.astype(F32))
    row = lax.broadcasted_iota(jnp.int32, p.shape, 0)
    up = jnp.where(row == 0, prev_row, pltpu.roll(p, 1, 0))
    dn = jnp.where(row == n - 1, next_row, pltpu.roll(p, n - 1, 0))
    return p + (0.5 * (up + dn) - p) * mu


def _head_sum(x, ones_bd):
    return _dot(x.astype(BF16), ones_bd)


def _rwkv_common(ps, kkw, ones_bd):
    r = ps[:, 0:BRANCH_W]
    k = ps[:, BRANCH_W:2 * BRANCH_W]
    v = ps[:, 2 * BRANCH_W:3 * BRANCH_W]
    kk = k * kkw
    kk = kk / jnp.maximum(jnp.sqrt(_head_sum(kk * kk, ones_bd)), 1e-12)
    return r, k, v, kk


def _rwkv_direction(ps, k, w0, w2pad, a0, a2pad, ka):
    z = w0 + _dot(jnp.tanh(ps[:, OFF_W1:OFF_W1 + 2 * W_LORA]).astype(BF16), w2pad)
    w_log = -(jnp.maximum(-z, 0.0) + jnp.log(1.0 + jnp.exp(-jnp.abs(z)))) - 0.5
    logw = -jnp.exp(w_log)
    a = jax.nn.sigmoid(a0 + _dot(ps[:, OFF_A1:OFF_A1 + 2 * A_LORA].astype(BF16), a2pad))
    kd = k * (1.0 + (a - 1.0) * ka)
    return logw, a, kd


def _stack_heads(y, bd):
    return jnp.where(bd, jnp.concatenate([y] * HEAD_PACK, axis=0), jnp.zeros((), y.dtype))


def _scan_chunk(r, kd, v, nk, b, cl, lw, g0, ms, mi, eye, bd, eye_w):
    half = SCAN_CHUNK // 2
    ref = cl[half:half + 1, :]
    tot = jnp.sum(lw, axis=0, keepdims=True)
    e_r = jnp.exp(cl - ref)
    e_p = jnp.exp(cl - lw - ref)
    e_i = jnp.exp(ref - cl)
    e_e = jnp.exp(tot - cl)
    g_ref = jnp.exp(ref)
    rt = r * e_r
    at = nk * e_p
    bt = (b * e_i).astype(BF16)
    kt = (kd * e_i).astype(BF16)
    bh = (b * e_e).astype(BF16)
    kh = (kd * e_e).astype(BF16)
    vb = v.astype(BF16)

    pmul = lambda x, y: _dot(x.astype(BF16), _stack_heads(y.astype(BF16), bd))

    lhs = jnp.concatenate([at, rt], axis=0).astype(BF16)
    rhs = jnp.concatenate([_stack_heads(bt, bd), _stack_heads(kt, bd)], axis=0)
    prod = _dot_nt(lhs, rhs)
    n = SCAN_CHUNK
    a_ab = prod[0:n, 0:PACK_W] * ms
    a_ak = prod[0:n, PACK_W:2 * PACK_W] * ms
    a_rb = prod[n:2 * n, 0:PACK_W] * mi
    a_rk = prod[n:2 * n, PACK_W:2 * PACK_W] * mi

    pw = a_ab
    tinv = eye + a_ab
    for _ in range(int(math.log2(SCAN_CHUNK)) - 1):
        pw = pmul(pw, pw)
        tinv = tinv + pmul(pw, tinv)

    akv = pmul(a_ak, vb)
    w = pmul(tinv, at * g_ref)
    u0 = pmul(tinv, akv)
    y0 = pmul(a_rb, u0) + pmul(a_rk, vb)
    rw = rt * g_ref + pmul(a_rb, w)

    wb = w.astype(BF16)
    mm = jnp.where(bd, _dot_tn(bh, wb), 0.0) + eye_w * jnp.exp(tot)
    cc = jnp.where(bd, _dot_tn(bh, u0.astype(BF16)) + _dot_tn(kh, vb), 0.0)

    g0b = g0.astype(BF16)
    y = y0 + _dot(rw.astype(BF16), g0b)
    g1 = _dot(mm.astype(BF16), g0b) + cc
    return y, g1


def _rwkv_scan_kernel(p_ref, pp_ref, pn_ref, mu_ref, kkw_ref, ka_ref, w0_ref, w2_ref, a0_ref, a2_ref,
                      ones_ref, tri_ref, y_ref,
                      r_s, kd_s, v_s, nk_s, b_s, cl_s, lw_s, g_s):
    d = pl.program_id(1)
    j = pl.program_id(2)
    nt = pl.num_programs(2)
    fwd = d == 0
    ti = jnp.where(fwd, j, nt - 1 - j)

    @pl.when(j == 0)
    def _():
        g_s[...] = jnp.zeros(g_s.shape, F32)

    ps = _token_shift(p_ref[0], pp_ref[0], pn_ref[0], ti == 0, ti == nt - 1, mu_ref[...])
    r, k, v, kk = _rwkv_common(ps, kkw_ref[...], ones_ref[...])
    logw, a, kd = _rwkv_direction(ps, k, w0_ref[0], w2_ref[0], a0_ref[0], a2_ref[0], ka_ref[...])
    lw_hi, lw_lo = _split_bf16(logw)
    tri = tri_ref[0]
    r_s[...] = r
    kd_s[...] = kd
    v_s[...] = v
    nk_s[...] = -kk
    b_s[...] = kk * a
    lw_s[...] = logw
    cl_s[...] = _dot(tri, lw_hi) + _dot(tri, lw_lo)

    n = SCAN_CHUNK
    t_idx = lax.broadcasted_iota(jnp.int32, (n, PACK_W), 0)
    s_idx = lax.broadcasted_iota(jnp.int32, (n, PACK_W), 1) & (n - 1)
    ahead = (t_idx - s_idx) * jnp.where(fwd, 1, -1)
    ms = (ahead > 0).astype(F32)
    eye = (s_idx == t_idx).astype(F32)
    mi = ms + eye
    ri = lax.broadcasted_iota(jnp.int32, (PACK_W, PACK_W), 0)
    ci = lax.broadcasted_iota(jnp.int32, (PACK_W, PACK_W), 1)
    bd = (ri >> 6) == (ci >> 6)
    eye_w = (ri == ci).astype(F32)

    nchunks = p_ref.shape[1] // n

    def body(c, carry):
        cidx = jnp.where(fwd, c, nchunks - 1 - c)
        rows = pl.ds(pl.multiple_of(cidx * n, n), n)
        for g in range(BRANCH_W // PACK_W):
            cols = slice(g * PACK_W, (g + 1) * PACK_W)
            y, g1 = _scan_chunk(r_s[rows, cols], kd_s[rows, cols], v_s[rows, cols], nk_s[rows, cols],
                                b_s[rows, cols], cl_s[rows, cols], lw_s[rows, cols], g_s[g],
                                ms, mi, eye, bd, eye_w)
            g_s[g] = g1
            y_ref[0, 0, rows, cols] = y
        return carry

    lax.fori_loop(0, nchunks, body, 0)


def _rwkv_scan(p3, wts, consts):
    b, s, _ = p3.shape
    ts = min(TOKEN_TILE, s)
    nt = s // ts
    r8 = ts // V7X_SUBLANES
    tile = lambda d, j: j + d * (nt - 1 - 2 * j)
    vec = lambda: _resident((1, BRANCH_W))
    per_dir = lambda shape: pl.BlockSpec((1,) + shape, lambda bi, d, j: (d,) + (0,) * len(shape))
    return pl.pallas_call(
        _rwkv_scan_kernel,
        grid=(b, 2, nt),
        in_specs=[pl.BlockSpec((1, ts, RWKV_COLS), lambda bi, d, j: (bi, tile(d, j), 0)),
                  pl.BlockSpec((1, V7X_SUBLANES, RWKV_COLS),
                               lambda bi, d, j: (bi, jnp.maximum(tile(d, j) * r8 - 1, 0), 0)),
                  pl.BlockSpec((1, V7X_SUBLANES, RWKV_COLS),
                               lambda bi, d, j: (bi, jnp.minimum((tile(d, j) + 1) * r8, nt * r8 - 1), 0)),
                  _resident((1, RWKV_COLS)), vec(), vec(),
                  per_dir((1, BRANCH_W)), per_dir((2 * W_LORA, BRANCH_W)),
                  per_dir((1, BRANCH_W)), per_dir((2 * A_LORA, BRANCH_W)),
                  _resident((BRANCH_W, BRANCH_W)), per_dir((ts, ts))],
        out_specs=pl.BlockSpec((1, 1, ts, BRANCH_W), lambda bi, d, j: (d, bi, tile(d, j), 0)),
        out_shape=jax.ShapeDtypeStruct((2, b, s, BRANCH_W), F32),
        scratch_shapes=[pltpu.VMEM((ts, BRANCH_W), F32)] * 7
                       + [pltpu.VMEM((BRANCH_W // PACK_W, PACK_W, PACK_W), F32)],
        compiler_params=_params("parallel", "parallel", "arbitrary"),
        name="rwkv7_scan",
    )(p3, p3, p3, wts["mu"], wts["kk"], wts["ka"], wts["w0"], wts["w2pad"], wts["a0"], wts["a2pad"],
      consts["ones_bd"], consts["tri"][ts])


def _rwkv_finish_kernel(p_ref, pp_ref, pn_ref, yf_ref, yb_ref, mu_ref, kkw_ref, ka_ref, rk_ref, lng_ref, lnb_ref,
                        w0_ref, w2_ref, a0_ref, a2_ref, g2_ref, ones_ref, o_ref, *, tps):
    i = pl.program_id(0)
    first = (i % tps) == 0
    last = (i % tps) == tps - 1
    ones_bd = ones_ref[...]
    ps = _token_shift(p_ref[...], pp_ref[...], pn_ref[...], first, last, mu_ref[...])
    r, k, v, _ = _rwkv_common(ps, kkw_ref[...], ones_bd)
    k_sum = jnp.zeros_like(k)
    for d in range(2):
        _, _, kd = _rwkv_direction(ps, k, w0_ref[d], w2_ref[d], a0_ref[d], a2_ref[d], ka_ref[...])
        k_sum = k_sum + kd
    y = yf_ref[0] + yb_ref[0]
    inv_hd = 1.0 / RWKV_HD
    ym = _head_sum(y, ones_bd) * inv_hd
    yc = y - ym
    yv = _head_sum(yc * yc, ones_bd) * inv_hd
    y = yc * lax.rsqrt(yv + LNX_EPS) * lng_ref[...] + lnb_ref[...]
    y = y + _head_sum(r * (0.5 * k_sum) * rk_ref[...], ones_bd) * v
    gate = _dot(jax.nn.sigmoid(ps[:, OFF_G1:OFF_G1 + G_LORA]).astype(BF16), g2_ref[...])
    o_ref[...] = (y * gate).astype(BF16)


def _rwkv_finish(p2, y_dirs, wts, consts, *, seq):
    t = p2.shape[0]
    ts = min(TOKEN_TILE, seq)
    tps = seq // ts
    r8 = ts // V7X_SUBLANES
    nblk8 = t // V7X_SUBLANES
    row = lambda i: (i, 0)
    vec = lambda: _resident((1, BRANCH_W))
    return pl.pallas_call(
        functools.partial(_rwkv_finish_kernel, tps=tps),
        grid=(t // ts,),
        in_specs=[pl.BlockSpec((ts, RWKV_COLS), row),
                  pl.BlockSpec((V7X_SUBLANES, RWKV_COLS), lambda i: (jnp.maximum(i * r8 - 1, 0), 0)),
                  pl.BlockSpec((V7X_SUBLANES, RWKV_COLS), lambda i: (jnp.minimum((i + 1) * r8, nblk8 - 1), 0)),
                  pl.BlockSpec((1, ts, BRANCH_W), lambda i: (0, i, 0)),
                  pl.BlockSpec((1, ts, BRANCH_W), lambda i: (1, i, 0)),
                  _resident((1, RWKV_COLS)), vec(), vec(), vec(), vec(), vec(),
                  _resident((2, 1, BRANCH_W)), _resident((2, 2 * W_LORA, BRANCH_W)),
                  _resident((2, 1, BRANCH_W)), _resident((2, 2 * A_LORA, BRANCH_W)),
                  _resident((G_LORA, BRANCH_W)), _resident((BRANCH_W, BRANCH_W))],
        out_specs=pl.BlockSpec((ts, BRANCH_W), row),
        out_shape=jax.ShapeDtypeStruct((t, BRANCH_W), BF16),
        compiler_params=_params("parallel"),
        name="rwkv7_finish",
    )(p2, p2, p2, y_dirs, y_dirs, wts["mu"], wts["kk"], wts["ka"], wts["rk"], wts["lnx_g"], wts["lnx_b"],
      wts["w0"], wts["w2pad"], wts["a0"], wts["a2pad"], wts["g2"], consts["ones_bd"])


def _diff_attn_kernel(q_ref, k_ref, v_ref, lam_ref, ng_ref, o_ref, *, lam_init):
    tq = q_ref.shape[1]
    s = k_ref.shape[1]
    q = q_ref[0]
    lane = lax.broadcasted_iota(jnp.int32, q.shape, 1)
    zero = jnp.zeros((), q.dtype)
    qq = jnp.concatenate([jnp.where(lane < DIFF_HD, q, zero), jnp.where(lane >= DIFF_HD, q, zero)], axis=0)
    ones_col = (lax.broadcasted_iota(jnp.int32, (ATT_K_TILE, DIFF_VD), 1) == 0).astype(BF16)

    def body(kb, carry):
        m_old, acc = carry
        rows = pl.ds(pl.multiple_of(kb * ATT_K_TILE, ATT_K_TILE), ATT_K_TILE)
        sc = _dot_nt(qq, k_ref[0, rows, :])
        m_new = jnp.maximum(m_old, jnp.max(sc, axis=1, keepdims=True))
        p = jnp.exp(sc - m_new).astype(BF16)
        v_ext = jnp.concatenate([v_ref[0, rows, :], ones_col], axis=1)
        acc = jnp.exp(m_old - m_new) * acc + _dot(p, v_ext)
        return m_new, acc

    m0 = jnp.full((2 * tq, 1), -1e30, F32)
    acc0 = jnp.zeros((2 * tq, 2 * DIFF_VD), F32)
    _, acc = lax.fori_loop(0, s // ATT_K_TILE, body, (m0, acc0))

    lam = lam_ref[...]
    lam_full = (jnp.exp(jnp.sum(lam[0:1] * lam[1:2], keepdims=True))
                - jnp.exp(jnp.sum(lam[2:3] * lam[3:4], keepdims=True)) + lam_init)
    o = acc[:, 0:DIFF_VD] / acc[:, DIFF_VD:DIFF_VD + 1]
    o = o[0:tq] - lam_full * o[tq:2 * tq]
    o = o * lax.rsqrt(jnp.mean(o * o, axis=-1, keepdims=True) + RMS_EPS) * ng_ref[...] * (1.0 - lam_init)
    o_ref[0] = o.astype(BF16)


def _diff_attention(pd3, lam, norm_g, *, lam_init):
    b, s, _ = pd3.shape
    tq = min(ATT_Q_TILE, s)
    h = DIFF_HEADS
    return pl.pallas_call(
        functools.partial(_diff_attn_kernel, lam_init=lam_init),
        grid=(b, h, s // tq),
        in_specs=[pl.BlockSpec((1, tq, 2 * DIFF_HD), lambda bi, hi, qi: (bi, qi, hi)),
                  pl.BlockSpec((1, s, 2 * DIFF_HD), lambda bi, hi, qi: (bi, 0, h + hi)),
                  pl.BlockSpec((1, s, DIFF_VD), lambda bi, hi, qi: (bi, 0, 2 * h + hi)),
                  _resident((4, DIFF_HD)), _resident((1, DIFF_VD))],
        out_specs=pl.BlockSpec((1, tq, DIFF_VD), lambda bi, hi, qi: (bi, qi, hi)),
        out_shape=jax.ShapeDtypeStruct((b, s, h * DIFF_VD), BF16),
        compiler_params=_params("parallel", "parallel", "arbitrary"),
        name="diff_attention",
    )(pd3, pd3, pd3, lam, norm_g)


def _fnet_stage1_kernel(y_ref, m_ref, o_ref):
    n1 = y_ref.shape[1]
    c = BRANCH_W
    pq = _dot(m_ref[...], y_ref[0])
    for t2 in range(FNET_COL_TILE):
        re = slice(2 * c * t2, 2 * c * t2 + c)
        im = slice(2 * c * t2 + c, 2 * c * (t2 + 1))
        o_ref[0, :, re] = (pq[0:n1, re] + pq[n1:2 * n1, im]).astype(BF16)
        o_ref[0, :, im] = (pq[0:n1, im] - pq[n1:2 * n1, re]).astype(BF16)


def _fnet_stage2_kernel(a_ref, e_ref, o_ref, *, scale):
    c = BRANCH_W
    n2 = FNET_N2
    for f in range(FNET_F1_TILE):
        res = _dot(e_ref[f, :, 0:n2], a_ref[0, f, :, 0:c]) + _dot(e_ref[f, :, n2:2 * n2], a_ref[0, f, :, c:2 * c])
        o_ref[0, :, f * c:(f + 1) * c] = (res * scale).astype(BF16)


def _fourier_mixer(yf3, consts):
    b, s, c2 = yf3.shape
    n2 = FNET_N2
    n1 = s // n2
    width = FNET_COL_TILE * c2
    stage1 = pl.pallas_call(
        _fnet_stage1_kernel,
        grid=(b, n2 * c2 // width),
        in_specs=[pl.BlockSpec((1, n1, width), lambda bi, j: (bi, 0, j)),
                  _resident((2 * n1, n1))],
        out_specs=pl.BlockSpec((1, n1, width), lambda bi, j: (bi, 0, j)),
        out_shape=jax.ShapeDtypeStruct((b, n1, n2 * c2), BF16),
        compiler_params=_params("parallel", "parallel"),
        name="fnet_stage1",
    )(yf3.reshape(b, n1, n2 * c2), consts["dft1"][s])
    a4 = stage1.reshape(b, n1, n2, c2)
    f1t = FNET_F1_TILE
    out = pl.pallas_call(
        functools.partial(_fnet_stage2_kernel, scale=float(1.0 / math.sqrt(s * FNET_GD))),
        grid=(b, n1 // f1t),
        in_specs=[pl.BlockSpec((1, f1t, n2, c2), lambda bi, j: (bi, j, 0, 0)),
                  pl.BlockSpec((f1t, n2, 2 * n2), lambda bi, j: (j, 0, 0))],
        out_specs=pl.BlockSpec((1, n2, f1t * BRANCH_W), lambda bi, j: (bi, 0, j)),
        out_shape=jax.ShapeDtypeStruct((b, n2, n1 * BRANCH_W), BF16),
        compiler_params=_params("parallel", "parallel"),
        name="fnet_stage2",
    )(a4, consts["dft2"][s])
    return out.reshape(b, s, BRANCH_W)


def _pool_branch(p, prev8, next8, first, last, pos0, seq, scale, ext_ref):
    tm = p.shape[0]
    halo = V7X_SUBLANES
    ext_ref[0:halo, :] = prev8 * (1.0 - first.astype(F32))
    ext_ref[halo:halo + tm, :] = p
    ext_ref[halo + tm:2 * halo + tm, :] = next8 * (1.0 - last.astype(F32))
    pos = pos0 + lax.broadcasted_iota(jnp.int32, (tm, 1), 0)
    outs = []
    for g, win in enumerate(POOL_WINDOWS):
        hw = win // 2
        cols = slice(g * POOL_GD, (g + 1) * POOL_GD)
        acc = ext_ref[halo - hw:halo - hw + tm, cols]
        for off in range(-hw + 1, hw):
            acc = acc + ext_ref[halo + off:halo + off + tm, cols]
        count = (jnp.minimum(pos + hw, seq) - jnp.maximum(pos - hw, 0)).astype(F32)
        outs.append(acc / count - p[:, cols])
    return jnp.concatenate(outs, axis=1) * scale


def _merge_kernel(x_ref, mod_ref, yr_ref, yd_ref, yf_ref, pp_ref, ppp_ref, ppn_ref, gate_ref,
                  wb_ref, wo_ref, ps_ref, g_ref, b_ref, o_ref, ext_ref, *, tps, seq, alpha):
    i = pl.program_id(0)
    tm = x_ref.shape[0]
    it = i % tps
    y_pool = _pool_branch(pp_ref[...], ppp_ref[...], ppn_ref[...], it == 0, it == tps - 1,
                          it * tm, seq, ps_ref[...], ext_ref)
    branches = (yr_ref[...], yd_ref[...], yf_ref[...], y_pool.astype(BF16))
    merged = jnp.zeros((tm, D_MODEL), F32)
    for n, br in enumerate(branches):
        gate = gate_ref[:, n * D_MODEL:(n + 1) * D_MODEL].astype(F32)
        merged = merged + gate * _dot(br, wb_ref[n])
    upd = _dot(merged.astype(BF16), wo_ref[...])
    mod = mod_ref[0]
    y = alpha * x_ref[...] + (1.0 + mod[5:6]) * upd
    o_ref[...] = _layer_norm(y) * g_ref[...] + b_ref[...]


def _merge_sublayer(x2, mod, y_rwkv, y_diff, y_fnet, p_pool, gates, w_branch, w_out, pool_scale, ln_g, ln_b,
                    *, seq, alpha):
    t = x2.shape[0]
    tm = min(TOKEN_TILE, seq)
    tps = seq // tm
    r8 = tm // V7X_SUBLANES
    nblk8 = t // V7X_SUBLANES
    row = lambda i: (i, 0)
    br = lambda: pl.BlockSpec((tm, BRANCH_W), row)
    return pl.pallas_call(
        functools.partial(_merge_kernel, tps=tps, seq=seq, alpha=alpha),
        grid=(t // tm,),
        in_specs=[pl.BlockSpec((tm, D_MODEL), row),
                  pl.BlockSpec((1, 9, D_MODEL), lambda i: (i // tps, 0, 0)),
                  br(), br(), br(), br(),
                  pl.BlockSpec((V7X_SUBLANES, BRANCH_W), lambda i: (jnp.maximum(i * r8 - 1, 0), 0)),
                  pl.BlockSpec((V7X_SUBLANES, BRANCH_W), lambda i: (jnp.minimum((i + 1) * r8, nblk8 - 1), 0)),
                  pl.BlockSpec((tm, N_BRANCH * D_MODEL), row),
                  _resident((N_BRANCH, BRANCH_W, D_MODEL)), _resident((D_MODEL, D_MODEL)),
                  _resident((1, BRANCH_W)), _resident((1, D_MODEL)), _resident((1, D_MODEL))],
        out_specs=pl.BlockSpec((tm, D_MODEL), row),
        out_shape=jax.ShapeDtypeStruct((t, D_MODEL), F32),
        scratch_shapes=[pltpu.VMEM((tm + 2 * V7X_SUBLANES, BRANCH_W), F32)],
        compiler_params=_params("parallel"),
        name="merge_sublayer",
    )(x2, mod, y_rwkv, y_diff, y_fnet, p_pool, p_pool, p_pool, gates, w_branch, w_out, pool_scale, ln_g, ln_b)


def _rope_tables(s):
    inv = ROPE_THETA ** (-jnp.arange(0, ROPE_DIM, 2, dtype=F32) / ROPE_DIM)
    ang = jnp.arange(s, dtype=F32)[:, None] * inv[None, :]
    cos, sin = jnp.cos(ang), jnp.sin(ang)
    half = ROPE_DIM // 2
    pad = jnp.zeros((s, DIFF_HD - ROPE_DIM), F32)
    zero = jnp.zeros((s, half), F32)
    c64 = jnp.concatenate([cos, cos, pad + 1.0], axis=1)
    s1_64 = jnp.concatenate([-sin, zero, pad], axis=1)
    s2_64 = jnp.concatenate([zero, sin, pad], axis=1)
    return tuple(jnp.concatenate([x, x], axis=1) for x in (c64, s1_64, s2_64))


def _group_dft_matrix():
    c = np.arange(FNET_GD)
    ang = 2.0 * np.pi * ((c[:, None] * c[None, :]) % FNET_GD) / FNET_GD
    eye = np.eye(FNET_GROUPS)
    return jnp.asarray(np.concatenate([np.kron(eye, np.cos(ang)), -np.kron(eye, np.sin(ang))], axis=1), BF16)


def _seq_dft_tables(s):
    n2 = FNET_N2
    n1 = s // n2
    a = np.arange(n1)
    ang1 = 2.0 * np.pi * ((a[:, None] * a[None, :]) % n1) / n1
    dft1 = np.concatenate([np.cos(ang1), np.sin(ang1)], axis=0)
    f = a[:, None, None] + n1 * np.arange(n2)[None, :, None]
    ang2 = 2.0 * np.pi * ((f * np.arange(n2)[None, None, :]) % s) / s
    dft2 = np.concatenate([np.cos(ang2), np.sin(ang2)], axis=2)
    return jnp.asarray(dft1, BF16), jnp.asarray(dft2, BF16)


def _scan_cumsum_matrices(ts):
    t = np.arange(ts)
    same = (t[:, None] // SCAN_CHUNK) == (t[None, :] // SCAN_CHUNK)
    fwd = same & (t[None, :] <= t[:, None])
    bwd = same & (t[None, :] >= t[:, None])
    return jnp.asarray(np.stack([fwd, bwd]).astype(np.float32), BF16)


def _constants(seqs):
    h = np.arange(BRANCH_W) // RWKV_HD
    consts = {
        "ones_bd": jnp.asarray((h[:, None] == h[None, :]).astype(np.float32), BF16),
        "fmat": _group_dft_matrix(),
        "rope": {}, "dft1": {}, "dft2": {}, "tri": {},
    }
    for s in seqs:
        consts["rope"][s] = _rope_tables(s)
        consts["dft1"][s], consts["dft2"][s] = _seq_dft_tables(s)
        ts = min(TOKEN_TILE, s)
        consts["tri"][ts] = _scan_cumsum_matrices(ts)
    return consts


def _pad_lora(w2):
    z = jnp.zeros_like(w2[0])
    return jnp.stack([jnp.concatenate([w2[0], z], axis=0), jnp.concatenate([z, w2[1]], axis=0)]).astype(BF16)


def _encoder_layer(x2, b, s, mod, lam_init, w, consts):
    alpha = (2.0 * DEPTH) ** 0.25
    x2 = _ffn_sublayer(x2, mod, w["ln_g"][0], w["ln_b"][0], w["ffa_in"], w["ffa_out"], sub=0, seq=s, alpha=alpha)
    p_rwkv, p_diff, p_fnet, p_pool, gates = _mixer_projection(x2, mod, w["w_in"], consts["rope"][s],
                                                               consts["fmat"], seq=s)
    y_dirs = _rwkv_scan(p_rwkv.reshape(b, s, RWKV_COLS), w["rwkv"], consts)
    y_rwkv = _rwkv_finish(p_rwkv, y_dirs.reshape(2, b * s, BRANCH_W), w["rwkv"], consts, seq=s)
    y_diff = _diff_attention(p_diff.reshape(b, s, DIFF_COLS), w["diff_lam"], w["diff_norm_g"], lam_init=lam_init)
    y_fnet = _fourier_mixer(p_fnet.reshape(b, s, 2 * BRANCH_W), consts)
    x2 = _merge_sublayer(x2, mod, y_rwkv, y_diff.reshape(b * s, BRANCH_W), y_fnet.reshape(b * s, BRANCH_W),
                         p_pool, gates, w["w_branch"], w["w_out"], w["pool_scale"], w["ln_g"][1], w["ln_b"][1],
                         seq=s, alpha=alpha)
    x2 = _ffn_sublayer(x2, mod, w["ln_g"][2], w["ln_b"][2], w["ffb_in"], w["ffb_out"], sub=2, seq=s, alpha=alpha)
    return x2


def kernel(x_prompt, x_sample, c_prompt, c_sample, ada_w, ada_b, ln_g, ln_b, ffa_w_in, ffa_w_out, ffb_w_in, ffb_w_out, w_in, rwkv_mu, rwkv_w0, rwkv_w2, rwkv_a0, rwkv_a2, rwkv_g2, rwkv_kk, rwkv_ka, rwkv_rk, rwkv_lnx_g, rwkv_lnx_b, diff_lam, diff_norm_g, pool_scale, w_branch, w_out):
    groups = ((x_prompt, c_prompt), (x_sample, c_sample))
    consts = _constants(sorted({x.shape[1] for x, _ in groups}))
    depth = ada_w.shape[0]

    nb = [c.shape[0] for _, c in groups]
    c_all = jnp.concatenate([c for _, c in groups], axis=0)
    rows = -(-c_all.shape[0] // V7X_SUBLANES) * V7X_SUBLANES
    c_pad = jnp.pad(c_all, ((0, rows - c_all.shape[0]), (0, 0)))

    layers = []
    for l in range(depth):
        mod_all = _modulation(c_pad, ada_w[l], ada_b[l]).reshape(rows, 9, D_MODEL)
        vec = lambda a: a[l].reshape(1, -1)
        layers.append({
            "mods": (mod_all[:nb[0]], mod_all[nb[0]:nb[0] + nb[1]]),
            "lam_init": 0.8 - 0.6 * math.exp(-0.3 * l),
            "ln_g": [ln_g[l, j].reshape(1, -1) for j in range(3)],
            "ln_b": [ln_b[l, j].reshape(1, -1) for j in range(3)],
            "ffa_in": ffa_w_in[l].astype(BF16), "ffa_out": ffa_w_out[l].astype(BF16),
            "ffb_in": ffb_w_in[l].astype(BF16), "ffb_out": ffb_w_out[l].astype(BF16),
            "w_in": w_in[l].astype(BF16),
            "w_branch": w_branch[l].astype(BF16), "w_out": w_out[l].astype(BF16),
            "pool_scale": vec(pool_scale),
            "diff_lam": diff_lam[l], "diff_norm_g": vec(diff_norm_g),
            "rwkv": {
                "mu": vec(rwkv_mu), "kk": vec(rwkv_kk), "ka": vec(rwkv_ka), "rk": vec(rwkv_rk),
                "lnx_g": vec(rwkv_lnx_g), "lnx_b": vec(rwkv_lnx_b),
                "w0": rwkv_w0[l].reshape(2, 1, BRANCH_W), "a0": rwkv_a0[l].reshape(2, 1, BRANCH_W),
                "w2pad": _pad_lora(rwkv_w2[l]), "a2pad": _pad_lora(rwkv_a2[l]),
                "g2": rwkv_g2[l].astype(BF16),
            },
        })

    outs = []
    for gi, (x, _) in enumerate(groups):
        b, s, d = x.shape
        x2 = x.reshape(b * s, d)
        for w in layers:
            x2 = _encoder_layer(x2, b, s, w["mods"][gi], w["lam_init"], w, consts)
        outs.append(x2.reshape(b, s, d))
    return tuple(outs)
```

```python
import functools
import math

import numpy as np
import jax
import jax.numpy as jnp
from jax import lax
from jax.experimental import pallas as pl
from jax.experimental.pallas import tpu as pltpu

D_MODEL = 1024
DEPTH = 2
BRANCH_W = 512
N_BRANCH = 4
RWKV_HEADS = 8
RWKV_HD = 64
W_LORA = 64
A_LORA = 64
G_LORA = 128
DIFF_HEADS = 4
DIFF_HD = 64
DIFF_VD = 128
ROPE_DIM = DIFF_HD // 4
ROPE_THETA = 500000.0
FNET_GROUPS = 4
FNET_GD = BRANCH_W // FNET_GROUPS
POOL_GROUPS = 4
POOL_GD = BRANCH_W // POOL_GROUPS
POOL_WINDOWS = (2, 4, 8, 16)
D_FF = 2816
LN_EPS = 1e-5
LNX_EPS = 64e-5
RMS_EPS = 1e-5

RWKV_COLS = 3 * BRANCH_W + 2 * W_LORA + 2 * A_LORA + G_LORA
DIFF_COLS = 2 * DIFF_HEADS * 2 * DIFF_HD + DIFF_HEADS * DIFF_VD
OFF_DIFF = RWKV_COLS
OFF_FNET = OFF_DIFF + DIFF_COLS
OFF_POOL = OFF_FNET + BRANCH_W
OFF_GATE = OFF_POOL + BRANCH_W
N_IN = OFF_GATE + N_BRANCH * D_MODEL
OFF_W1 = 3 * BRANCH_W
OFF_A1 = OFF_W1 + 2 * W_LORA
OFF_G1 = OFF_A1 + 2 * A_LORA

V7X_SUBLANES = 8
V7X_LANES = 128
V7X_VMEM_LIMIT_BYTES = 56 * 1024 * 1024

TOKEN_TILE = 512
FF_CHUNK = 256
SCAN_CHUNK = 64
SCAN_INTERLEAVE = 4
HEAD_PACK = 4
PACK_W = HEAD_PACK * RWKV_HD
ATT_Q_TILE = 256
ATT_K_TILE = 512
ATT_UNROLL = 8
FNET_N2 = 128
FNET_F1_TILE = 8
FNET_COL_TILE = 8

LOG2_E = 1.4426950408889634

BF16 = jnp.bfloat16
F32 = jnp.float32


def _dot(a, b):
    return jnp.dot(a, b, preferred_element_type=F32)


def _dot_nt(a, b):
    return lax.dot_general(a, b, (((1,), (1,)), ((), ())), preferred_element_type=F32)


def _dot_tn(a, b):
    return lax.dot_general(a, b, (((0,), (0,)), ((), ())), preferred_element_type=F32)


def _split_bf16(x):
    hi = x.astype(BF16)
    lo = (x - hi.astype(F32)).astype(BF16)
    return hi, lo


def _layer_norm(x, eps=LN_EPS):
    mu = jnp.mean(x, axis=-1, keepdims=True)
    xc = x - mu
    var = jnp.mean(xc * xc, axis=-1, keepdims=True)
    return xc * lax.rsqrt(var + eps)


def _params(*semantics):
    return pltpu.CompilerParams(dimension_semantics=semantics, vmem_limit_bytes=V7X_VMEM_LIMIT_BYTES)


def _resident(shape):
    nd = len(shape)
    return pl.BlockSpec(shape, lambda *_: (0,) * nd, pipeline_mode=pl.Buffered(1))


def _mod_kernel(c_ref, w_ref, b_ref, o_ref):
    c = c_ref[...]
    a = c * jax.nn.sigmoid(c)
    a_hi, a_lo = _split_bf16(a)
    w_hi, w_lo = _split_bf16(w_ref[...])
    o_ref[...] = _dot(a_hi, w_hi) + _dot(a_lo, w_hi) + _dot(a_hi, w_lo) + b_ref[...]


def _modulation(c_pad, ada_w, ada_b):
    rows = c_pad.shape[0]
    n = ada_w.shape[1]
    nb = 9 * V7X_LANES
    return pl.pallas_call(
        _mod_kernel,
        grid=(n // nb,),
        in_specs=[pl.BlockSpec((rows, D_MODEL), lambda i: (0, 0)),
                  pl.BlockSpec((D_MODEL, nb), lambda i: (0, i)),
                  pl.BlockSpec((1, nb), lambda i: (0, i))],
        out_specs=pl.BlockSpec((rows, nb), lambda i: (0, i)),
        out_shape=jax.ShapeDtypeStruct((rows, n), F32),
        compiler_params=_params("arbitrary"),
        name="adaln_mod",
    )(c_pad, ada_w, ada_b.reshape(1, n))


def _ffn_kernel(x_ref, mod_ref, g_ref, b_ref, win_ref, wout_ref, o_ref, *, sub, alpha):
    x = x_ref[...]
    mod = mod_ref[0]
    shift, scale, gate = mod[3 * sub:3 * sub + 1], mod[3 * sub + 1:3 * sub + 2], mod[3 * sub + 2:3 * sub + 3]
    h = (_layer_norm(x) * (1.0 + scale) + shift).astype(BF16)
    acc = jnp.zeros(x.shape, F32)
    for c in range(D_FF // FF_CHUNK):
        lo = c * FF_CHUNK
        ua = _dot(h, win_ref[:, lo:lo + FF_CHUNK])
        ub = _dot(h, win_ref[:, D_FF + lo:D_FF + lo + FF_CHUNK])
        act = (ua * jax.nn.sigmoid(ua) * ub).astype(BF16)
        acc = acc + _dot(act, wout_ref[lo:lo + FF_CHUNK, :])
    y = alpha * x + (1.0 + gate) * (0.5 * acc)
    o_ref[...] = _layer_norm(y) * g_ref[...] + b_ref[...]


def _ffn_sublayer(x2, mod, ln_g, ln_b, w_in, w_out, *, sub, seq, alpha):
    t = x2.shape[0]
    tm = min(TOKEN_TILE, seq)
    tps = seq // tm
    return pl.pallas_call(
        functools.partial(_ffn_kernel, sub=sub, alpha=alpha),
        grid=(t // tm,),
        in_specs=[pl.BlockSpec((tm, D_MODEL), lambda i: (i, 0)),
                  pl.BlockSpec((1, 9, D_MODEL), lambda i: (i // tps, 0, 0)),
                  _resident((1, D_MODEL)), _resident((1, D_MODEL)),
                  _resident((D_MODEL, 2 * D_FF)), _resident((D_FF, D_MODEL))],
        out_specs=pl.BlockSpec((tm, D_MODEL), lambda i: (i, 0)),
        out_shape=jax.ShapeDtypeStruct((t, D_MODEL), F32),
        compiler_params=_params("parallel"),
        name=f"ffn_sublayer{sub}",
    )(x2, mod, ln_g, ln_b, w_in, w_out)


def _proj_kernel(x_ref, mod_ref, w_ref, rc_ref, rs1_ref, rs2_ref, fmat_ref,
                 o_rwkv, o_diff, o_fnet, o_pool, o_gate):
    x = x_ref[...]
    mod = mod_ref[0]
    h = (_layer_norm(x) * (1.0 + mod[4:5]) + mod[3:4]).astype(BF16)

    for lo in range(0, RWKV_COLS, 640):
        o_rwkv[:, lo:lo + 640] = _dot(h, w_ref[:, lo:lo + 640])

    rc = jnp.concatenate([rc_ref[...]] * 4, axis=1)
    rs1 = jnp.concatenate([rs1_ref[...]] * 4, axis=1)
    rs2 = jnp.concatenate([rs2_ref[...]] * 4, axis=1)
    for blk, scl in ((0, DIFF_HD ** -0.5 * LOG2_E), (1, 1.0)):
        lo = OFF_DIFF + blk * 512
        u = _dot(h, w_ref[:, lo:lo + 512])
        rot = u * rc + pltpu.roll(u, 512 - ROPE_DIM // 2, 1) * rs1 + pltpu.roll(u, ROPE_DIM // 2, 1) * rs2
        o_diff[:, blk * 512:(blk + 1) * 512] = (rot * scl).astype(BF16)
    lo = OFF_DIFF + 1024
    o_diff[:, 1024:1536] = _dot(h, w_ref[:, lo:lo + 512]).astype(BF16)

    pf = _dot(h, w_ref[:, OFF_FNET:OFF_FNET + BRANCH_W]).astype(BF16)
    o_fnet[...] = _dot(pf, fmat_ref[...]).astype(BF16)

    o_pool[...] = _dot(h, w_ref[:, OFF_POOL:OFF_POOL + BRANCH_W])

    for c in range(N_BRANCH * D_MODEL // 512):
        lo = OFF_GATE + c * 512
        o_gate[:, c * 512:(c + 1) * 512] = jax.nn.sigmoid(_dot(h, w_ref[:, lo:lo + 512])).astype(BF16)


def _mixer_projection(x2, mod, w_in, rope, fmat, *, seq):
    t = x2.shape[0]
    tm = min(TOKEN_TILE, seq)
    tps = seq // tm
    row = lambda i: (i, 0)
    rope_spec = pl.BlockSpec((tm, V7X_LANES), lambda i: (i % tps, 0))
    return pl.pallas_call(
        _proj_kernel,
        grid=(t // tm,),
        in_specs=[pl.BlockSpec((tm, D_MODEL), row),
                  pl.BlockSpec((1, 9, D_MODEL), lambda i: (i // tps, 0, 0)),
                  _resident((D_MODEL, N_IN)),
                  rope_spec, rope_spec, rope_spec,
                  _resident((BRANCH_W, 2 * BRANCH_W))],
        out_specs=[pl.BlockSpec((tm, RWKV_COLS), row),
                   pl.BlockSpec((tm, DIFF_COLS), row),
                   pl.BlockSpec((tm, 2 * BRANCH_W), row),
                   pl.BlockSpec((tm, BRANCH_W), row),
                   pl.BlockSpec((tm, N_BRANCH * D_MODEL), row)],
        out_shape=[jax.ShapeDtypeStruct((t, RWKV_COLS), F32),
                   jax.ShapeDtypeStruct((t, DIFF_COLS), BF16),
                   jax.ShapeDtypeStruct((t, 2 * BRANCH_W), BF16),
                   jax.ShapeDtypeStruct((t, BRANCH_W), F32),
                   jax.ShapeDtypeStruct((t, N_BRANCH * D_MODEL), BF16)],
        compiler_params=_params("parallel"),
        name="mixer_projection",
    )(x2, mod, w_in, rope[0], rope[1], rope[2], fmat)


def _token_shift(p, prev8, next8, first, last, mu):
    n = p.shape[0]
    prev_row = prev8[V7X_SUBLANES - 1:V7X_SUBLANES, :] * (1.0 - first.astype(F32))
    next_row = next8[0:1, :] * (1.0 - last.astype(F32))
    row = lax.broadcasted_iota(jnp.int32, p.shape, 0)
    up = jnp.where(row == 0, prev_row, pltpu.roll(p, 1, 0))
    dn = jnp.where(row == n - 1, next_row, pltpu.roll(p, n - 1, 0))
    return p + (0.5 * (up + dn) - p) * mu


def _head_sum(x, ones_bd):
    return _dot(x.astype(BF16), ones_bd)


def _rwkv_common(ps, kkw, ones_bd):
    r = ps[:, 0:BRANCH_W]
    k = ps[:, BRANCH_W:2 * BRANCH_W]
    v = ps[:, 2 * BRANCH_W:3 * BRANCH_W]
    kk = k * kkw
    kk = kk / jnp.maximum(jnp.sqrt(_head_sum(kk * kk, ones_bd)), 1e-12)
    return r, k, v, kk


def _rwkv_direction(ps, k, w0, w2pad, a0, a2pad, ka):
    z = w0 + _dot(jnp.tanh(ps[:, OFF_W1:OFF_W1 + 2 * W_LORA]).astype(BF16), w2pad)
    w_log = -(jnp.maximum(-z, 0.0) + jnp.log(1.0 + jnp.exp(-jnp.abs(z)))) - 0.5
    logw = -jnp.exp(w_log)
    a = jax.nn.sigmoid(a0 + _dot(ps[:, OFF_A1:OFF_A1 + 2 * A_LORA].astype(BF16), a2pad))
    kd = k * (1.0 + (a - 1.0) * ka)
    return logw, a, kd


def _rwkv_prep_kernel(p_ref, pp_ref, pn_ref, mu_ref, kkw_ref, ka_ref, rk_ref, w0_ref, w2_ref, a0_ref, a2_ref,
                      g2_ref, ones_ref, tri_ref, r_o, v_o, nk_o, kd_o, b_o, cl_o, lw_o, bonus_o, gate_o, *, tps):
    i = pl.program_id(0)
    first = (i % tps) == 0
    last = (i % tps) == tps - 1
    ones_bd = ones_ref[...]
    ps = _token_shift(p_ref[...], pp_ref[...], pn_ref[...], first, last, mu_ref[...])
    r, k, v, kk = _rwkv_common(ps, kkw_ref[...], ones_bd)
    r_o[...] = r.astype(BF16)
    v_o[...] = v.astype(BF16)
    nk_o[...] = (-kk).astype(BF16)
    k_sum = jnp.zeros_like(k)
    for d in range(2):
        logw, a, kd = _rwkv_direction(ps, k, w0_ref[d], w2_ref[d], a0_ref[d], a2_ref[d], ka_ref[...])
        kd_o[d] = kd.astype(BF16)
        b_o[d] = (kk * a).astype(BF16)
        lw_o[d] = logw
        lw_hi, lw_lo = _split_bf16(logw)
        cl_o[d] = _dot(tri_ref[d], lw_hi) + _dot(tri_ref[d], lw_lo)
        k_sum = k_sum + kd
    bonus_o[...] = _head_sum(r * (0.5 * k_sum) * rk_ref[...], ones_bd) * v
    gate_o[...] = _dot(jax.nn.sigmoid(ps[:, OFF_G1:OFF_G1 + G_LORA]).astype(BF16), g2_ref[...])


def _rwkv_prep(p2, wts, consts, *, seq):
    t = p2.shape[0]
    ts = min(TOKEN_TILE, seq)
    tps = seq // ts
    r8 = ts // V7X_SUBLANES
    nblk8 = t // V7X_SUBLANES
    row = lambda i: (i, 0)
    vec = lambda: _resident((1, BRANCH_W))
    tok = lambda dt: jax.ShapeDtypeStruct((t, BRANCH_W), dt)
    both = lambda dt: jax.ShapeDtypeStruct((2, t, BRANCH_W), dt)
    tok_spec = pl.BlockSpec((ts, BRANCH_W), row)
    both_spec = pl.BlockSpec((2, ts, BRANCH_W), lambda i: (0, i, 0))
    return pl.pallas_call(
        functools.partial(_rwkv_prep_kernel, tps=tps),
        grid=(t // ts,),
        in_specs=[pl.BlockSpec((ts, RWKV_COLS), row),
                  pl.BlockSpec((V7X_SUBLANES, RWKV_COLS), lambda i: (jnp.maximum(i * r8 - 1, 0), 0)),
                  pl.BlockSpec((V7X_SUBLANES, RWKV_COLS), lambda i: (jnp.minimum((i + 1) * r8, nblk8 - 1), 0)),
                  _resident((1, RWKV_COLS)), vec(), vec(), vec(),
                  _resident((2, 1, BRANCH_W)), _resident((2, 2 * W_LORA, BRANCH_W)),
                  _resident((2, 1, BRANCH_W)), _resident((2, 2 * A_LORA, BRANCH_W)),
                  _resident((G_LORA, BRANCH_W)), _resident((BRANCH_W, BRANCH_W)), _resident((2, ts, ts))],
        out_specs=[tok_spec, tok_spec, tok_spec, both_spec, both_spec, both_spec, both_spec, tok_spec, tok_spec],
        out_shape=[tok(BF16), tok(BF16), tok(BF16), both(BF16), both(BF16), both(F32), both(F32), tok(F32), tok(F32)],
        compiler_params=_params("parallel"),
        name="rwkv7_prep",
    )(p2, p2, p2, wts["mu"], wts["kk"], wts["ka"], wts["rk"], wts["w0"], wts["w2pad"], wts["a0"], wts["a2pad"],
      wts["g2"], consts["ones_bd"], consts["tri"][ts])


def _stack_heads(y, bd):
    return jnp.where(bd, jnp.concatenate([y] * HEAD_PACK, axis=0), jnp.zeros((), y.dtype))


def _chunk_local(insts, ms, mi, eye, bd, eye_w, fwd):
    n = SCAN_CHUNK
    half = n // 2
    stk = lambda y: _stack_heads(y.astype(BF16), bd)
    each = lambda f, *lists: [f(*xs) for xs in zip(*lists)]

    pre = []
    for r, kd, v, nk, b, cl, lw in insts:
        ref = cl[half:half + 1, :]
        tot = jnp.where(fwd, cl[n - 1:n, :], cl[0:1, :])
        e_i = jnp.exp(ref - cl)
        e_e = jnp.exp(tot - cl)
        g_ref = jnp.exp(ref)
        rt = r * jnp.exp(cl - ref)
        at = nk * jnp.exp(cl - lw - ref)
        pre.append(dict(rt=rt, at=at, bt=(b * e_i).astype(BF16), kt=(kd * e_i).astype(BF16),
                        bh=(b * e_e).astype(BF16), kh=(kd * e_e).astype(BF16), vb=v.astype(BF16),
                        g_ref=g_ref, g_tot=jnp.exp(tot)))

    prod = [_dot_nt(jnp.concatenate([p["at"], p["rt"]], axis=0).astype(BF16),
                    jnp.concatenate([stk(p["bt"]), stk(p["kt"])], axis=0)) for p in pre]
    a_ab = [x[0:n, 0:PACK_W] * ms for x in prod]
    a_rb = [x[n:2 * n, 0:PACK_W] * mi for x in prod]
    a_kk = [jnp.concatenate([x[0:n, PACK_W:2 * PACK_W] * ms, x[n:2 * n, PACK_W:2 * PACK_W] * mi], axis=0)
            for x in prod]
    av = [_dot(a.astype(BF16), stk(p["vb"])) for a, p in zip(a_kk, pre)]
    khv = [_dot_tn(p["kh"], p["vb"]) for p in pre]

    steps = int(math.log2(n))
    pw = each(lambda a: _dot(a.astype(BF16), stk(a)), a_ab)
    tinv = [eye + a for a in a_ab]
    for k in range(1, steps):
        if k < steps - 1:
            res = each(lambda a, t: _dot(a.astype(BF16), jnp.concatenate([stk(a), stk(t)], axis=1)), pw, tinv)
            pw = [x[:, 0:PACK_W] for x in res]
            tinv = each(lambda t, x: t + x[:, PACK_W:2 * PACK_W], tinv, res)
        else:
            tinv = each(lambda a, t: t + _dot(a.astype(BF16), stk(t)), pw, tinv)
    wu = each(lambda t, p, x: _dot(t.astype(BF16), jnp.concatenate([stk(p["at"] * p["g_ref"]), stk(x[0:n])], axis=1)),
              tinv, pre, av)
    w = [x[:, 0:PACK_W] for x in wu]
    u = [x[:, PACK_W:2 * PACK_W] for x in wu]

    res1 = each(lambda a, b_, c: _dot(a.astype(BF16), jnp.concatenate([stk(b_), stk(c)], axis=1)), a_rb, u, w)
    res2 = each(lambda p, b_, c: _dot_tn(p["bh"], jnp.concatenate([b_, c], axis=1).astype(BF16)), pre, w, u)
    outs = []
    for p, x1, x2, avx, kv in zip(pre, res1, res2, av, khv):
        y0 = x1[:, 0:PACK_W] + avx[n:2 * n]
        rw = p["rt"] * p["g_ref"] + x1[:, PACK_W:2 * PACK_W]
        mm = jnp.where(bd, x2[:, 0:PACK_W], 0.0) + eye_w * p["g_tot"]
        cc = jnp.where(bd, x2[:, PACK_W:2 * PACK_W] + kv, 0.0)
        outs.append((y0, rw.astype(BF16), mm.astype(BF16), cc))
    return outs


def _rwkv_scan_kernel(r_ref, v_ref, nk_ref, kd_ref, b_ref, cl_ref, lw_ref, y_ref, y0_s, rw_s, mm_s, cc_s, g_s):
    d = pl.program_id(1)
    j = pl.program_id(2)
    fwd = d == 0

    @pl.when(j == 0)
    def _():
        g_s[...] = jnp.zeros(g_s.shape, F32)

    n = SCAN_CHUNK
    ngroups = BRANCH_W // PACK_W
    nchunks = r_ref.shape[1] // n
    t_idx = lax.broadcasted_iota(jnp.int32, (n, PACK_W), 0)
    s_idx = lax.broadcasted_iota(jnp.int32, (n, PACK_W), 1) & (n - 1)
    ahead = (t_idx - s_idx) * jnp.where(fwd, 1, -1)
    ms = (ahead > 0).astype(F32)
    mi = (ahead >= 0).astype(F32)
    eye = (ahead == 0).astype(F32)
    ri = lax.broadcasted_iota(jnp.int32, (PACK_W, PACK_W), 0)
    ci = lax.broadcasted_iota(jnp.int32, (PACK_W, PACK_W), 1)
    bd = (ri >> 6) == (ci >> 6)
    eye_w = (ri == ci).astype(F32)

    def local(i, carry):
        insts, where = [], []
        for k in range(SCAN_INTERLEAVE):
            c = i * SCAN_INTERLEAVE + k
            rows = pl.ds(pl.multiple_of(c * n, n), n)
            for g in range(ngroups):
                cols = slice(g * PACK_W, (g + 1) * PACK_W)
                ld = lambda ref: ref[0, rows, cols].astype(F32)
                insts.append((ld(r_ref), kd_ref[0, 0, rows, cols].astype(F32), ld(v_ref), ld(nk_ref),
                              b_ref[0, 0, rows, cols].astype(F32), cl_ref[0, 0, rows, cols],
                              lw_ref[0, 0, rows, cols]))
                where.append((c, g, rows, cols))
        outs = _chunk_local(insts, ms, mi, eye, bd, eye_w, fwd)
        for (c, g, rows, cols), (y0, rw, mm, cc) in zip(where, outs):
            y0_s[rows, cols] = y0
            rw_s[rows, cols] = rw
            mm_s[c, g] = mm
            cc_s[c, g] = cc
        return carry

    lax.fori_loop(0, nchunks // SCAN_INTERLEAVE, local, 0)

    def chain(c, carry):
        cidx = jnp.where(fwd, c, nchunks - 1 - c)
        rows = pl.ds(pl.multiple_of(cidx * n, n), n)
        new = []
        for g in range(ngroups):
            cols = slice(g * PACK_W, (g + 1) * PACK_W)
            res = _dot(jnp.concatenate([rw_s[rows, cols], mm_s[cidx, g]], axis=0), carry[g].astype(BF16))
            y_ref[0, 0, rows, cols] = y0_s[rows, cols] + res[0:n]
            new.append(res[n:] + cc_s[cidx, g])
        return tuple(new)

    g_fin = lax.fori_loop(0, nchunks, chain, tuple(g_s[g] for g in range(ngroups)))
    for g in range(ngroups):
        g_s[g] = g_fin[g]


def _rwkv_scan(prep, b, s):
    r, v, nk, kd, bb, cl, lw = prep
    ts = min(TOKEN_TILE, s)
    nt = s // ts
    nchunks = ts // SCAN_CHUNK
    assert nchunks % SCAN_INTERLEAVE == 0
    ngroups = BRANCH_W // PACK_W
    tile = lambda d, j: j + d * (nt - 1 - 2 * j)
    tok = pl.BlockSpec((1, ts, BRANCH_W), lambda bi, d, j: (bi, tile(d, j), 0))
    per_dir = pl.BlockSpec((1, 1, ts, BRANCH_W), lambda bi, d, j: (d, bi, tile(d, j), 0))
    r3 = lambda a: a.reshape(b, s, BRANCH_W)
    r4 = lambda a: a.reshape(2, b, s, BRANCH_W)
    return pl.pallas_call(
        _rwkv_scan_kernel,
        grid=(b, 2, nt),
        in_specs=[tok, tok, tok, per_dir, per_dir, per_dir, per_dir],
        out_specs=per_dir,
        out_shape=jax.ShapeDtypeStruct((2, b, s, BRANCH_W), F32),
        scratch_shapes=[pltpu.VMEM((ts, BRANCH_W), F32), pltpu.VMEM((ts, BRANCH_W), BF16),
                        pltpu.VMEM((nchunks, ngroups, PACK_W, PACK_W), BF16),
                        pltpu.VMEM((nchunks, ngroups, PACK_W, PACK_W), F32),
                        pltpu.VMEM((ngroups, PACK_W, PACK_W), F32)],
        compiler_params=_params("parallel", "parallel", "arbitrary"),
        name="rwkv7_scan",
    )(r3(r), r3(v), r3(nk), r4(kd), r4(bb), r4(cl), r4(lw))


def _rwkv_finish_kernel(yf_ref, yb_ref, bonus_ref, gate_ref, lng_ref, lnb_ref, ones_ref, o_ref):
    ones_bd = ones_ref[...]
    y = yf_ref[0] + yb_ref[0]
    inv_hd = 1.0 / RWKV_HD
    ym = _head_sum(y, ones_bd) * inv_hd
    yc = y - ym
    yv = _head_sum(yc * yc, ones_bd) * inv_hd
    y = yc * lax.rsqrt(yv + LNX_EPS) * lng_ref[...] + lnb_ref[...] + bonus_ref[...]
    o_ref[...] = (y * gate_ref[...]).astype(BF16)


def _rwkv_finish(y_dirs, bonus, gate, wts, consts, *, seq):
    t = bonus.shape[0]
    ts = min(TOKEN_TILE, seq)
    row = lambda i: (i, 0)
    vec = lambda: _resident((1, BRANCH_W))
    tok = pl.BlockSpec((ts, BRANCH_W), row)
    return pl.pallas_call(
        _rwkv_finish_kernel,
        grid=(t // ts,),
        in_specs=[pl.BlockSpec((1, ts, BRANCH_W), lambda i: (0, i, 0)),
                  pl.BlockSpec((1, ts, BRANCH_W), lambda i: (1, i, 0)),
                  tok, tok, vec(), vec(), _resident((BRANCH_W, BRANCH_W))],
        out_specs=tok,
        out_shape=jax.ShapeDtypeStruct((t, BRANCH_W), BF16),
        compiler_params=_params("parallel"),
        name="rwkv7_finish",
    )(y_dirs, y_dirs, bonus, gate, wts["lnx_g"], wts["lnx_b"], consts["ones_bd"])


def _diff_attn_kernel(q_ref, k_ref, v_ref, lam_ref, ng_ref, o_ref, *, lam_init):
    tq = q_ref.shape[1]
    s = k_ref.shape[1]
    q = q_ref[0]
    lane = lax.broadcasted_iota(jnp.int32, q.shape, 1)
    zero = jnp.zeros((), q.dtype)
    qq = jnp.concatenate([jnp.where(lane < DIFF_HD, q, zero), jnp.where(lane >= DIFF_HD, q, zero)], axis=0)
    ones_col = (lax.broadcasted_iota(jnp.int32, (ATT_K_TILE, DIFF_VD), 1) == 0).astype(BF16)

    def body(kb, carry):
        m_old, acc = carry
        rows = pl.ds(pl.multiple_of(kb * ATT_K_TILE, ATT_K_TILE), ATT_K_TILE)
        sc = _dot_nt(qq, k_ref[0, rows, :])
        m_new = jnp.maximum(m_old, jnp.max(sc, axis=1, keepdims=True))
        p = jnp.exp2(sc - m_new).astype(BF16)
        v_ext = jnp.concatenate([v_ref[0, rows, :], ones_col], axis=1)
        acc = jnp.exp2(m_old - m_new) * acc + _dot(p, v_ext)
        return m_new, acc

    m0 = jnp.full((2 * tq, 1), -1e30, F32)
    acc0 = jnp.zeros((2 * tq, 2 * DIFF_VD), F32)
    _, acc = lax.fori_loop(0, s // ATT_K_TILE, body, (m0, acc0), unroll=ATT_UNROLL)

    lam = lam_ref[...]
    lam_full = (jnp.exp(jnp.sum(lam[0:1] * lam[1:2], keepdims=True))
                - jnp.exp(jnp.sum(lam[2:3] * lam[3:4], keepdims=True)) + lam_init)
    o = acc[:, 0:DIFF_VD] / acc[:, DIFF_VD:DIFF_VD + 1]
    o = o[0:tq] - lam_full * o[tq:2 * tq]
    o = o * lax.rsqrt(jnp.mean(o * o, axis=-1, keepdims=True) + RMS_EPS) * ng_ref[...] * (1.0 - lam_init)
    o_ref[0] = o.astype(BF16)


def _diff_attention(pd3, lam, norm_g, *, lam_init):
    b, s, _ = pd3.shape
    tq = min(ATT_Q_TILE, s)
    h = DIFF_HEADS
    return pl.pallas_call(
        functools.partial(_diff_attn_kernel, lam_init=lam_init),
        grid=(b, h, s // tq),
        in_specs=[pl.BlockSpec((1, tq, 2 * DIFF_HD), lambda bi, hi, qi: (bi, qi, hi)),
                  pl.BlockSpec((1, s, 2 * DIFF_HD), lambda bi, hi, qi: (bi, 0, h + hi)),
                  pl.BlockSpec((1, s, DIFF_VD), lambda bi, hi, qi: (bi, 0, 2 * h + hi)),
                  _resident((4, DIFF_HD)), _resident((1, DIFF_VD))],
        out_specs=pl.BlockSpec((1, tq, DIFF_VD), lambda bi, hi, qi: (bi, qi, hi)),
        out_shape=jax.ShapeDtypeStruct((b, s, h * DIFF_VD), BF16),
        compiler_params=_params("parallel", "parallel", "arbitrary"),
        name="diff_attention",
    )(pd3, pd3, pd3, lam, norm_g)


def _fnet_stage1_kernel(y_ref, m_ref, o_ref):
    n1 = y_ref.shape[1]
    c = BRANCH_W
    pq = _dot(m_ref[...], y_ref[0])
    for t2 in range(FNET_COL_TILE):
        re = slice(2 * c * t2, 2 * c * t2 + c)
        im = slice(2 * c * t2 + c, 2 * c * (t2 + 1))
        o_ref[0, :, re] = (pq[0:n1, re] + pq[n1:2 * n1, im]).astype(BF16)
        o_ref[0, :, im] = (pq[0:n1, im] - pq[n1:2 * n1, re]).astype(BF16)


def _fnet_stage2_kernel(a_ref, e_ref, o_ref, *, scale):
    c = BRANCH_W
    n2 = FNET_N2
    for f in range(FNET_F1_TILE):
        res = _dot(e_ref[f, :, 0:n2], a_ref[0, f, :, 0:c]) + _dot(e_ref[f, :, n2:2 * n2], a_ref[0, f, :, c:2 * c])
        o_ref[0, :, f * c:(f + 1) * c] = (res * scale).astype(BF16)


def _fourier_mixer(yf3, consts):
    b, s, c2 = yf3.shape
    n2 = FNET_N2
    n1 = s // n2
    width = FNET_COL_TILE * c2
    stage1 = pl.pallas_call(
        _fnet_stage1_kernel,
        grid=(b, n2 * c2 // width),
        in_specs=[pl.BlockSpec((1, n1, width), lambda bi, j: (bi, 0, j)),
                  _resident((2 * n1, n1))],
        out_specs=pl.BlockSpec((1, n1, width), lambda bi, j: (bi, 0, j)),
        out_shape=jax.ShapeDtypeStruct((b, n1, n2 * c2), BF16),
        compiler_params=_params("parallel", "parallel"),
        name="fnet_stage1",
    )(yf3.reshape(b, n1, n2 * c2), consts["dft1"][s])
    a4 = stage1.reshape(b, n1, n2, c2)
    f1t = FNET_F1_TILE
    out = pl.pallas_call(
        functools.partial(_fnet_stage2_kernel, scale=float(1.0 / math.sqrt(s * FNET_GD))),
        grid=(b, n1 // f1t),
        in_specs=[pl.BlockSpec((1, f1t, n2, c2), lambda bi, j: (bi, j, 0, 0)),
                  pl.BlockSpec((f1t, n2, 2 * n2), lambda bi, j: (j, 0, 0))],
        out_specs=pl.BlockSpec((1, n2, f1t * BRANCH_W), lambda bi, j: (bi, 0, j)),
        out_shape=jax.ShapeDtypeStruct((b, n2, n1 * BRANCH_W), BF16),
        compiler_params=_params("parallel", "parallel"),
        name="fnet_stage2",
    )(a4, consts["dft2"][s])
    return out.reshape(b, s, BRANCH_W)


def _pool_branch(p, prev8, next8, first, last, pos0, seq, scale, ext_ref):
    tm = p.shape[0]
    halo = V7X_SUBLANES
    ext_ref[0:halo, :] = prev8 * (1.0 - first.astype(F32))
    ext_ref[halo:halo + tm, :] = p
    ext_ref[halo + tm:2 * halo + tm, :] = next8 * (1.0 - last.astype(F32))
    pos = pos0 + lax.broadcasted_iota(jnp.int32, (tm, 1), 0)
    outs = []
    for g, win in enumerate(POOL_WINDOWS):
        hw = win // 2
        cols = slice(g * POOL_GD, (g + 1) * POOL_GD)
        acc = ext_ref[halo - hw:halo - hw + tm, cols]
        for off in range(-hw + 1, hw):
            acc = acc + ext_ref[halo + off:halo + off + tm, cols]
        count = (jnp.minimum(pos + hw, seq) - jnp.maximum(pos - hw, 0)).astype(F32)
        outs.append(acc / count - p[:, cols])
    return jnp.concatenate(outs, axis=1) * scale


def _merge_kernel(x_ref, mod_ref, yr_ref, yd_ref, yf_ref, pp_ref, ppp_ref, ppn_ref, gate_ref,
                  wb_ref, wo_ref, ps_ref, g_ref, b_ref, o_ref, ext_ref, *, tps, seq, alpha):
    i = pl.program_id(0)
    tm = x_ref.shape[0]
    it = i % tps
    y_pool = _pool_branch(pp_ref[...], ppp_ref[...], ppn_ref[...], it == 0, it == tps - 1,
                          it * tm, seq, ps_ref[...], ext_ref)
    branches = (yr_ref[...], yd_ref[...], yf_ref[...], y_pool.astype(BF16))
    merged = jnp.zeros((tm, D_MODEL), F32)
    for n, br in enumerate(branches):
        gate = gate_ref[:, n * D_MODEL:(n + 1) * D_MODEL].astype(F32)
        merged = merged + gate * _dot(br, wb_ref[n])
    upd = _dot(merged.astype(BF16), wo_ref[...])
    mod = mod_ref[0]
    y = alpha * x_ref[...] + (1.0 + mod[5:6]) * upd
    o_ref[...] = _layer_norm(y) * g_ref[...] + b_ref[...]


def _merge_sublayer(x2, mod, y_rwkv, y_diff, y_fnet, p_pool, gates, w_branch, w_out, pool_scale, ln_g, ln_b,
                    *, seq, alpha):
    t = x2.shape[0]
    tm = min(TOKEN_TILE, seq)
    tps = seq // tm
    r8 = tm // V7X_SUBLANES
    nblk8 = t // V7X_SUBLANES
    row = lambda i: (i, 0)
    br = lambda: pl.BlockSpec((tm, BRANCH_W), row)
    return pl.pallas_call(
        functools.partial(_merge_kernel, tps=tps, seq=seq, alpha=alpha),
        grid=(t // tm,),
        in_specs=[pl.BlockSpec((tm, D_MODEL), row),
                  pl.BlockSpec((1, 9, D_MODEL), lambda i: (i // tps, 0, 0)),
                  br(), br(), br(), br(),
                  pl.BlockSpec((V7X_SUBLANES, BRANCH_W), lambda i: (jnp.maximum(i * r8 - 1, 0), 0)),
                  pl.BlockSpec((V7X_SUBLANES, BRANCH_W), lambda i: (jnp.minimum((i + 1) * r8, nblk8 - 1), 0)),
                  pl.BlockSpec((tm, N_BRANCH * D_MODEL), row),
                  _resident((N_BRANCH, BRANCH_W, D_MODEL)), _resident((D_MODEL, D_MODEL)),
                  _resident((1, BRANCH_W)), _resident((1, D_MODEL)), _resident((1, D_MODEL))],
        out_specs=pl.BlockSpec((tm, D_MODEL), row),
        out_shape=jax.ShapeDtypeStruct((t, D_MODEL), F32),
        scratch_shapes=[pltpu.VMEM((tm + 2 * V7X_SUBLANES, BRANCH_W), F32)],
        compiler_params=_params("parallel"),
        name="merge_sublayer",
    )(x2, mod, y_rwkv, y_diff, y_fnet, p_pool, p_pool, p_pool, gates, w_branch, w_out, pool_scale, ln_g, ln_b)


def _rope_tables(s):
    inv = ROPE_THETA ** (-jnp.arange(0, ROPE_DIM, 2, dtype=F32) / ROPE_DIM)
    ang = jnp.arange(s, dtype=F32)[:, None] * inv[None, :]
    cos, sin = jnp.cos(ang), jnp.sin(ang)
    half = ROPE_DIM // 2
    pad = jnp.zeros((s, DIFF_HD - ROPE_DIM), F32)
    zero = jnp.zeros((s, half), F32)
    c64 = jnp.concatenate([cos, cos, pad + 1.0], axis=1)
    s1_64 = jnp.concatenate([-sin, zero, pad], axis=1)
    s2_64 = jnp.concatenate([zero, sin, pad], axis=1)
    return tuple(jnp.concatenate([x, x], axis=1) for x in (c64, s1_64, s2_64))


def _group_dft_matrix():
    c = np.arange(FNET_GD)
    ang = 2.0 * np.pi * ((c[:, None] * c[None, :]) % FNET_GD) / FNET_GD
    eye = np.eye(FNET_GROUPS)
    return jnp.asarray(np.concatenate([np.kron(eye, np.cos(ang)), -np.kron(eye, np.sin(ang))], axis=1), BF16)


def _seq_dft_tables(s):
    n2 = FNET_N2
    n1 = s // n2
    a = np.arange(n1)
    ang1 = 2.0 * np.pi * ((a[:, None] * a[None, :]) % n1) / n1
    dft1 = np.concatenate([np.cos(ang1), np.sin(ang1)], axis=0)
    f = a[:, None, None] + n1 * np.arange(n2)[None, :, None]
    ang2 = 2.0 * np.pi * ((f * np.arange(n2)[None, None, :]) % s) / s
    dft2 = np.concatenate([np.cos(ang2), np.sin(ang2)], axis=2)
    return jnp.asarray(dft1, BF16), jnp.asarray(dft2, BF16)


def _scan_cumsum_matrices(ts):
    t = np.arange(ts)
    same = (t[:, None] // SCAN_CHUNK) == (t[None, :] // SCAN_CHUNK)
    fwd = same & (t[None, :] <= t[:, None])
    bwd = same & (t[None, :] >= t[:, None])
    return jnp.asarray(np.stack([fwd, bwd]).astype(np.float32), BF16)


def _constants(seqs):
    h = np.arange(BRANCH_W) // RWKV_HD
    consts = {
        "ones_bd": jnp.asarray((h[:, None] == h[None, :]).astype(np.float32), BF16),
        "fmat": _group_dft_matrix(),
        "rope": {}, "dft1": {}, "dft2": {}, "tri": {},
    }
    for s in seqs:
        consts["rope"][s] = _rope_tables(s)
        consts["dft1"][s], consts["dft2"][s] = _seq_dft_tables(s)
        ts = min(TOKEN_TILE, s)
        consts["tri"][ts] = _scan_cumsum_matrices(ts)
    return consts


def _pad_lora(w2):
    z = jnp.zeros_like(w2[0])
    return jnp.stack([jnp.concatenate([w2[0], z], axis=0), jnp.concatenate([z, w2[1]], axis=0)]).astype(BF16)


def _encoder_layer(x2, b, s, mod, lam_init, w, consts):
    alpha = (2.0 * DEPTH) ** 0.25
    x2 = _ffn_sublayer(x2, mod, w["ln_g"][0], w["ln_b"][0], w["ffa_in"], w["ffa_out"], sub=0, seq=s, alpha=alpha)
    p_rwkv, p_diff, p_fnet, p_pool, gates = _mixer_projection(x2, mod, w["w_in"], consts["rope"][s],
                                                               consts["fmat"], seq=s)
    prep = _rwkv_prep(p_rwkv, w["rwkv"], consts, seq=s)
    y_dirs = _rwkv_scan(prep[:7], b, s)
    y_rwkv = _rwkv_finish(y_dirs.reshape(2, b * s, BRANCH_W), prep[7], prep[8], w["rwkv"], consts, seq=s)
    y_diff = _diff_attention(p_diff.reshape(b, s, DIFF_COLS), w["diff_lam"], w["diff_norm_g"], lam_init=lam_init)
    y_fnet = _fourier_mixer(p_fnet.reshape(b, s, 2 * BRANCH_W), consts)
    x2 = _merge_sublayer(x2, mod, y_rwkv, y_diff.reshape(b * s, BRANCH_W), y_fnet.reshape(b * s, BRANCH_W),
                         p_pool, gates, w["w_branch"], w["w_out"], w["pool_scale"], w["ln_g"][1], w["ln_b"][1],
                         seq=s, alpha=alpha)
    x2 = _ffn_sublayer(x2, mod, w["ln_g"][2], w["ln_b"][2], w["ffb_in"], w["ffb_out"], sub=2, seq=s, alpha=alpha)
    return x2


def kernel(x_prompt, x_sample, c_prompt, c_sample, ada_w, ada_b, ln_g, ln_b, ffa_w_in, ffa_w_out, ffb_w_in, ffb_w_out, w_in, rwkv_mu, rwkv_w0, rwkv_w2, rwkv_a0, rwkv_a2, rwkv_g2, rwkv_kk, rwkv_ka, rwkv_rk, rwkv_lnx_g, rwkv_lnx_b, diff_lam, diff_norm_g, pool_scale, w_branch, w_out):
    groups = ((x_prompt, c_prompt), (x_sample, c_sample))
    consts = _constants(sorted({x.shape[1] for x, _ in groups}))
    depth = ada_w.shape[0]

    nb = [c.shape[0] for _, c in groups]
    c_all = jnp.concatenate([c for _, c in groups], axis=0)
    rows = -(-c_all.shape[0] // V7X_SUBLANES) * V7X_SUBLANES
    c_pad = jnp.pad(c_all, ((0, rows - c_all.shape[0]), (0, 0)))

    layers = []
    for l in range(depth):
        mod_all = _modulation(c_pad, ada_w[l], ada_b[l]).reshape(rows, 9, D_MODEL)
        vec = lambda a: a[l].reshape(1, -1)
        layers.append({
            "mods": (mod_all[:nb[0]], mod_all[nb[0]:nb[0] + nb[1]]),
            "lam_init": 0.8 - 0.6 * math.exp(-0.3 * l),
            "ln_g": [ln_g[l, j].reshape(1, -1) for j in range(3)],
            "ln_b": [ln_b[l, j].reshape(1, -1) for j in range(3)],
            "ffa_in": ffa_w_in[l].astype(BF16), "ffa_out": ffa_w_out[l].astype(BF16),
            "ffb_in": ffb_w_in[l].astype(BF16), "ffb_out": ffb_w_out[l].astype(BF16),
            "w_in": w_in[l].astype(BF16),
            "w_branch": w_branch[l].astype(BF16), "w_out": w_out[l].astype(BF16),
            "pool_scale": vec(pool_scale),
            "diff_lam": diff_lam[l], "diff_norm_g": vec(diff_norm_g),
            "rwkv": {
                "mu": vec(rwkv_mu), "kk": vec(rwkv_kk), "ka": vec(rwkv_ka), "rk": vec(rwkv_rk),
                "lnx_g": vec(rwkv_lnx_g), "lnx_b": vec(rwkv_lnx_b),
                "w0": rwkv_w0[l].reshape(2, 1, BRANCH_W), "a0": rwkv_a0[l].reshape(2, 1, BRANCH_W),
                "w2pad": _pad_lora(rwkv_w2[l]), "a2pad": _pad_lora(rwkv_a2[l]),
                "g2": rwkv_g2[l].astype(BF16),
            },
        })

    outs = []
    for gi, (x, _) in enumerate(groups):
        b, s, d = x.shape
        x2 = x.reshape(b * s, d)
        for w in layers:
            x2 = _encoder_layer(x2, b, s, w["mods"][gi], w["lam_init"], w, consts)
        outs.append(x2.reshape(b, s, d))
    return tuple(outs)
```

```python
import functools
import math

import numpy as np
import jax
import jax.numpy as jnp
from jax import lax
from jax.experimental import pallas as pl
from jax.experimental.pallas import tpu as pltpu

D_MODEL = 1024
DEPTH = 2
BRANCH_W = 512
N_BRANCH = 4
RWKV_HEADS = 8
RWKV_HD = 64
W_LORA = 64
A_LORA = 64
G_LORA = 128
DIFF_HEADS = 4
DIFF_HD = 64
DIFF_VD = 128
ROPE_DIM = DIFF_HD // 4
ROPE_THETA = 500000.0
FNET_GROUPS = 4
FNET_GD = BRANCH_W // FNET_GROUPS
POOL_GROUPS = 4
POOL_GD = BRANCH_W // POOL_GROUPS
POOL_WINDOWS = (2, 4, 8, 16)
D_FF = 2816
LN_EPS = 1e-5
LNX_EPS = 64e-5
RMS_EPS = 1e-5

RWKV_COLS = 3 * BRANCH_W + 2 * W_LORA + 2 * A_LORA + G_LORA
DIFF_COLS = 2 * DIFF_HEADS * 2 * DIFF_HD + DIFF_HEADS * DIFF_VD
OFF_DIFF = RWKV_COLS
OFF_FNET = OFF_DIFF + DIFF_COLS
OFF_POOL = OFF_FNET + BRANCH_W
OFF_GATE = OFF_POOL + BRANCH_W
N_IN = OFF_GATE + N_BRANCH * D_MODEL
OFF_W1 = 3 * BRANCH_W
OFF_A1 = OFF_W1 + 2 * W_LORA
OFF_G1 = OFF_A1 + 2 * A_LORA

V7X_SUBLANES = 8
V7X_LANES = 128
V7X_VMEM_LIMIT_BYTES = 56 * 1024 * 1024

TOKEN_TILE = 512
FF_CHUNK = 256
SCAN_TILE = 1024
SCAN_CHUNK = 64
SCAN_INTERLEAVE = 4
HEAD_PACK = 4
PACK_W = HEAD_PACK * RWKV_HD
ATT_Q_TILE = 512
ATT_K_TILE = 512
FNET_N2 = 128
FNET_F1_TILE = 8
FNET_COL_TILE = 8

LOG2_E = 1.4426950408889634

BF16 = jnp.bfloat16
F32 = jnp.float32


def _dot(a, b):
    return jnp.dot(a, b, preferred_element_type=F32)


def _dot_nt(a, b):
    return lax.dot_general(a, b, (((1,), (1,)), ((), ())), preferred_element_type=F32)


def _dot_tn(a, b):
    return lax.dot_general(a, b, (((0,), (0,)), ((), ())), preferred_element_type=F32)


def _split_bf16(x):
    hi = x.astype(BF16)
    lo = (x - hi.astype(F32)).astype(BF16)
    return hi, lo


def _layer_norm(x, eps=LN_EPS):
    mu = jnp.mean(x, axis=-1, keepdims=True)
    xc = x - mu
    var = jnp.mean(xc * xc, axis=-1, keepdims=True)
    return xc * lax.rsqrt(var + eps)


def _params(*semantics):
    return pltpu.CompilerParams(dimension_semantics=semantics, vmem_limit_bytes=V7X_VMEM_LIMIT_BYTES)


def _resident(shape):
    nd = len(shape)
    return pl.BlockSpec(shape, lambda *_: (0,) * nd, pipeline_mode=pl.Buffered(1))


def _mod_kernel(c_ref, w_ref, b_ref, o_ref):
    c = c_ref[...]
    a = c * jax.nn.sigmoid(c)
    a_hi, a_lo = _split_bf16(a)
    w_hi, w_lo = _split_bf16(w_ref[...])
    o_ref[...] = _dot(a_hi, w_hi) + _dot(a_lo, w_hi) + _dot(a_hi, w_lo) + b_ref[...]


def _modulation(c_pad, ada_w, ada_b):
    rows = c_pad.shape[0]
    n = ada_w.shape[1]
    nb = 9 * V7X_LANES
    return pl.pallas_call(
        _mod_kernel,
        grid=(n // nb,),
        in_specs=[pl.BlockSpec((rows, D_MODEL), lambda i: (0, 0)),
                  pl.BlockSpec((D_MODEL, nb), lambda i: (0, i)),
                  pl.BlockSpec((1, nb), lambda i: (0, i))],
        out_specs=pl.BlockSpec((rows, nb), lambda i: (0, i)),
        out_shape=jax.ShapeDtypeStruct((rows, n), F32),
        compiler_params=_params("arbitrary"),
        name="adaln_mod",
    )(c_pad, ada_w, ada_b.reshape(1, n))


def _ffn_kernel(x_ref, mod_ref, g_ref, b_ref, win_ref, wout_ref, o_ref, *, sub, alpha):
    x = x_ref[...]
    mod = mod_ref[0]
    shift, scale, gate = mod[3 * sub:3 * sub + 1], mod[3 * sub + 1:3 * sub + 2], mod[3 * sub + 2:3 * sub + 3]
    h = (_layer_norm(x) * (1.0 + scale) + shift).astype(BF16)
    acc = jnp.zeros(x.shape, F32)
    for c in range(D_FF // FF_CHUNK):
        lo = c * FF_CHUNK
        ua = _dot(h, win_ref[:, lo:lo + FF_CHUNK])
        ub = _dot(h, win_ref[:, D_FF + lo:D_FF + lo + FF_CHUNK])
        act = (ua * jax.nn.sigmoid(ua) * ub).astype(BF16)
        acc = acc + _dot(act, wout_ref[lo:lo + FF_CHUNK, :])
    y = alpha * x + (1.0 + gate) * (0.5 * acc)
    o_ref[...] = _layer_norm(y) * g_ref[...] + b_ref[...]


def _ffn_sublayer(x2, mod, ln_g, ln_b, w_in, w_out, *, sub, seq, alpha):
    t = x2.shape[0]
    tm = min(TOKEN_TILE, seq)
    tps = seq // tm
    return pl.pallas_call(
        functools.partial(_ffn_kernel, sub=sub, alpha=alpha),
        grid=(t // tm,),
        in_specs=[pl.BlockSpec((tm, D_MODEL), lambda i: (i, 0)),
                  pl.BlockSpec((1, 9, D_MODEL), lambda i: (i // tps, 0, 0)),
                  _resident((1, D_MODEL)), _resident((1, D_MODEL)),
                  _resident((D_MODEL, 2 * D_FF)), _resident((D_FF, D_MODEL))],
        out_specs=pl.BlockSpec((tm, D_MODEL), lambda i: (i, 0)),
        out_shape=jax.ShapeDtypeStruct((t, D_MODEL), F32),
        compiler_params=_params("parallel"),
        name=f"ffn_sublayer{sub}",
    )(x2, mod, ln_g, ln_b, w_in, w_out)


def _proj_kernel(x_ref, mod_ref, w_ref, rc_ref, rs1_ref, rs2_ref, fmat_ref,
                 o_rwkv, o_qt, o_k, o_vt, o_fnet, o_pool, o_gate):
    x = x_ref[...]
    mod = mod_ref[0]
    h = (_layer_norm(x) * (1.0 + mod[4:5]) + mod[3:4]).astype(BF16)

    for lo in range(0, RWKV_COLS, 640):
        o_rwkv[:, lo:lo + 640] = _dot(h, w_ref[:, lo:lo + 640])

    rc = jnp.concatenate([rc_ref[...]] * 4, axis=1)
    rs1 = jnp.concatenate([rs1_ref[...]] * 4, axis=1)
    rs2 = jnp.concatenate([rs2_ref[...]] * 4, axis=1)
    hw = 2 * DIFF_HD
    for blk, scl in ((0, DIFF_HD ** -0.5 * LOG2_E), (1, 1.0)):
        lo = OFF_DIFF + blk * 512
        u = _dot(h, w_ref[:, lo:lo + 512])
        rot = u * rc + pltpu.roll(u, 512 - ROPE_DIM // 2, 1) * rs1 + pltpu.roll(u, ROPE_DIM // 2, 1) * rs2
        if blk == 0:
            rot = rot * scl
            for hd in range(DIFF_HEADS):
                o_qt[0, hd] = rot[:, hd * hw:(hd + 1) * hw].T.astype(BF16)
        else:
            o_k[...] = rot.astype(BF16)
    lo = OFF_DIFF + 1024
    vv = _dot(h, w_ref[:, lo:lo + 512])
    for hd in range(DIFF_HEADS):
        o_vt[0, hd] = vv[:, hd * DIFF_VD:(hd + 1) * DIFF_VD].T.astype(BF16)

    pf = _dot(h, w_ref[:, OFF_FNET:OFF_FNET + BRANCH_W]).astype(BF16)
    o_fnet[...] = _dot(pf, fmat_ref[...]).astype(BF16)

    o_pool[...] = _dot(h, w_ref[:, OFF_POOL:OFF_POOL + BRANCH_W])

    for c in range(N_BRANCH * D_MODEL // 512):
        lo = OFF_GATE + c * 512
        o_gate[:, c * 512:(c + 1) * 512] = jax.nn.sigmoid(_dot(h, w_ref[:, lo:lo + 512])).astype(BF16)


def _mixer_projection(x2, mod, w_in, rope, fmat, *, seq):
    t = x2.shape[0]
    tm = min(TOKEN_TILE, seq)
    tps = seq // tm
    row = lambda i: (i, 0)
    rope_spec = pl.BlockSpec((tm, V7X_LANES), lambda i: (i % tps, 0))
    nbatch = t // seq
    head_t = pl.BlockSpec((1, DIFF_HEADS, 2 * DIFF_HD, tm), lambda i: (i // tps, 0, 0, i % tps))
    head_t_shape = jax.ShapeDtypeStruct((nbatch, DIFF_HEADS, 2 * DIFF_HD, seq), BF16)
    return pl.pallas_call(
        _proj_kernel,
        grid=(t // tm,),
        in_specs=[pl.BlockSpec((tm, D_MODEL), row),
                  pl.BlockSpec((1, 9, D_MODEL), lambda i: (i // tps, 0, 0)),
                  _resident((D_MODEL, N_IN)),
                  rope_spec, rope_spec, rope_spec,
                  _resident((BRANCH_W, 2 * BRANCH_W))],
        out_specs=[pl.BlockSpec((tm, RWKV_COLS), row),
                   head_t, pl.BlockSpec((tm, BRANCH_W), row), head_t,
                   pl.BlockSpec((tm, 2 * BRANCH_W), row),
                   pl.BlockSpec((tm, BRANCH_W), row),
                   pl.BlockSpec((tm, N_BRANCH * D_MODEL), row)],
        out_shape=[jax.ShapeDtypeStruct((t, RWKV_COLS), F32),
                   head_t_shape, jax.ShapeDtypeStruct((t, BRANCH_W), BF16), head_t_shape,
                   jax.ShapeDtypeStruct((t, 2 * BRANCH_W), BF16),
                   jax.ShapeDtypeStruct((t, BRANCH_W), F32),
                   jax.ShapeDtypeStruct((t, N_BRANCH * D_MODEL), BF16)],
        compiler_params=_params("parallel"),
        name="mixer_projection",
    )(x2, mod, w_in, rope[0], rope[1], rope[2], fmat)


def _token_shift(p, prev8, next8, first, last, mu):
    n = p.shape[0]
    prev_row = prev8[V7X_SUBLANES - 1:V7X_SUBLANES, :] * (1.0 - first.astype(F32))
    next_row = next8[0:1, :] * (1.0 - last.astype(F32))
    row = lax.broadcasted_iota(jnp.int32, p.shape, 0)
    up = jnp.where(row == 0, prev_row, pltpu.roll(p, 1, 0))
    dn = jnp.where(row == n - 1, next_row, pltpu.roll(p, n - 1, 0))
    return p + (0.5 * (up + dn) - p) * mu


def _head_sum(x, ones_bd):
    return _dot(x.astype(BF16), ones_bd)


def _rwkv_common(ps, kkw, ones_bd):
    r = ps[:, 0:BRANCH_W]
    k = ps[:, BRANCH_W:2 * BRANCH_W]
    v = ps[:, 2 * BRANCH_W:3 * BRANCH_W]
    kk = k * kkw
    kk = kk / jnp.maximum(jnp.sqrt(_head_sum(kk * kk, ones_bd)), 1e-12)
    return r, k, v, kk


def _rwkv_direction(ps, k, w0, w2pad, a0, a2pad, ka):
    z = w0 + _dot(jnp.tanh(ps[:, OFF_W1:OFF_W1 + 2 * W_LORA]).astype(BF16), w2pad)
    w_log = -(jnp.maximum(-z, 0.0) + jnp.log(1.0 + jnp.exp(-jnp.abs(z)))) - 0.5
    logw = -jnp.exp(w_log)
    a = jax.nn.sigmoid(a0 + _dot(ps[:, OFF_A1:OFF_A1 + 2 * A_LORA].astype(BF16), a2pad))
    kd = k * (1.0 + (a - 1.0) * ka)
    return logw, a, kd


def _rwkv_prep_kernel(p_ref, pp_ref, pn_ref, mu_ref, kkw_ref, ka_ref, rk_ref, w0_ref, w2_ref, a0_ref, a2_ref,
                      g2_ref, ones_ref, tri_ref, r_o, v_o, nk_o, kd_o, b_o, cl_o, lw_o, bonus_o, gate_o, *, tps):
    i = pl.program_id(0)
    first = (i % tps) == 0
    last = (i % tps) == tps - 1
    ones_bd = ones_ref[...]
    ps = _token_shift(p_ref[...], pp_ref[...], pn_ref[...], first, last, mu_ref[...])
    r, k, v, kk = _rwkv_common(ps, kkw_ref[...], ones_bd)
    r_o[...] = r.astype(BF16)
    v_o[...] = v.astype(BF16)
    nk_o[...] = (-kk).astype(BF16)
    k_sum = jnp.zeros_like(k)
    for d in range(2):
        logw, a, kd = _rwkv_direction(ps, k, w0_ref[d], w2_ref[d], a0_ref[d], a2_ref[d], ka_ref[...])
        kd_o[d] = kd.astype(BF16)
        b_o[d] = (kk * a).astype(BF16)
        lw_o[d] = logw
        lw_hi, lw_lo = _split_bf16(logw)
        cl_o[d] = _dot(tri_ref[d], lw_hi) + _dot(tri_ref[d], lw_lo)
        k_sum = k_sum + kd
    bonus_o[...] = _head_sum(r * (0.5 * k_sum) * rk_ref[...], ones_bd) * v
    gate_o[...] = _dot(jax.nn.sigmoid(ps[:, OFF_G1:OFF_G1 + G_LORA]).astype(BF16), g2_ref[...])


def _rwkv_prep(p2, wts, consts, *, seq):
    t = p2.shape[0]
    ts = min(TOKEN_TILE, seq)
    tps = seq // ts
    r8 = ts // V7X_SUBLANES
    nblk8 = t // V7X_SUBLANES
    row = lambda i: (i, 0)
    vec = lambda: _resident((1, BRANCH_W))
    tok = lambda dt: jax.ShapeDtypeStruct((t, BRANCH_W), dt)
    both = lambda dt: jax.ShapeDtypeStruct((2, t, BRANCH_W), dt)
    tok_spec = pl.BlockSpec((ts, BRANCH_W), row)
    both_spec = pl.BlockSpec((2, ts, BRANCH_W), lambda i: (0, i, 0))
    return pl.pallas_call(
        functools.partial(_rwkv_prep_kernel, tps=tps),
        grid=(t // ts,),
        in_specs=[pl.BlockSpec((ts, RWKV_COLS), row),
                  pl.BlockSpec((V7X_SUBLANES, RWKV_COLS), lambda i: (jnp.maximum(i * r8 - 1, 0), 0)),
                  pl.BlockSpec((V7X_SUBLANES, RWKV_COLS), lambda i: (jnp.minimum((i + 1) * r8, nblk8 - 1), 0)),
                  _resident((1, RWKV_COLS)), vec(), vec(), vec(),
                  _resident((2, 1, BRANCH_W)), _resident((2, 2 * W_LORA, BRANCH_W)),
                  _resident((2, 1, BRANCH_W)), _resident((2, 2 * A_LORA, BRANCH_W)),
                  _resident((G_LORA, BRANCH_W)), _resident((BRANCH_W, BRANCH_W)), _resident((2, ts, ts))],
        out_specs=[tok_spec, tok_spec, tok_spec, both_spec, both_spec, both_spec, both_spec, tok_spec, tok_spec],
        out_shape=[tok(BF16), tok(BF16), tok(BF16), both(BF16), both(BF16), both(F32), both(F32), tok(F32), tok(F32)],
        compiler_params=_params("parallel"),
        name="rwkv7_prep",
    )(p2, p2, p2, wts["mu"], wts["kk"], wts["ka"], wts["rk"], wts["w0"], wts["w2pad"], wts["a0"], wts["a2pad"],
      wts["g2"], consts["ones_bd"], consts["tri"][ts])


def _stack_heads(y, bd):
    return jnp.where(bd, jnp.concatenate([y] * HEAD_PACK, axis=0), jnp.zeros((), y.dtype))


def _chunk_local(insts, ms, mi, eye, bd, eye_w, fwd, side=()):
    side = list(side)
    spacer = lambda: side.pop(0)() if side else None
    n = SCAN_CHUNK
    half = n // 2
    stk = lambda y: _stack_heads(y.astype(BF16), bd)
    each = lambda f, *lists: [f(*xs) for xs in zip(*lists)]

    pre = []
    for r, kd, v, nk, b, cl, lw in insts:
        ref = cl[half:half + 1, :]
        tot = jnp.where(fwd, cl[n - 1:n, :], cl[0:1, :])
        e_i = jnp.exp(ref - cl)
        e_e = jnp.exp(tot - cl)
        g_ref = jnp.exp(ref)
        rt = r * jnp.exp(cl - ref)
        at = nk * jnp.exp(cl - lw - ref)
        pre.append(dict(rt=rt, at=at, bt=(b * e_i).astype(BF16), kt=(kd * e_i).astype(BF16),
                        bh=(b * e_e).astype(BF16), kh=(kd * e_e).astype(BF16), vb=v.astype(BF16),
                        g_ref=g_ref, g_tot=jnp.exp(tot)))

    prod = [_dot_nt(jnp.concatenate([p["at"], p["rt"]], axis=0).astype(BF16),
                    jnp.concatenate([stk(p["bt"]), stk(p["kt"])], axis=0)) for p in pre]
    spacer()
    a_ab = [x[0:n, 0:PACK_W] * ms for x in prod]
    a_rb = [x[n:2 * n, 0:PACK_W] * mi for x in prod]
    a_kk = [jnp.concatenate([x[0:n, PACK_W:2 * PACK_W] * ms, x[n:2 * n, PACK_W:2 * PACK_W] * mi], axis=0)
            for x in prod]
    av = [_dot(a.astype(BF16), stk(p["vb"])) for a, p in zip(a_kk, pre)]
    khv = [_dot_tn(p["kh"], p["vb"]) for p in pre]

    steps = int(math.log2(n))
    pw = each(lambda a: _dot(a.astype(BF16), stk(a)), a_ab)
    tinv = [eye + a for a in a_ab]
    for k in range(1, steps):
        if k % 2 == 1:
            spacer()
        if k < steps - 1:
            res = each(lambda a, t: _dot(a.astype(BF16), jnp.concatenate([stk(a), stk(t)], axis=1)), pw, tinv)
            pw = [x[:, 0:PACK_W] for x in res]
            tinv = each(lambda t, x: t + x[:, PACK_W:2 * PACK_W], tinv, res)
        else:
            tinv = each(lambda a, t: t + _dot(a.astype(BF16), stk(t)), pw, tinv)
    wu = each(lambda t, p, x: _dot(t.astype(BF16), jnp.concatenate([stk(p["at"] * p["g_ref"]), stk(x[0:n])], axis=1)),
              tinv, pre, av)
    w = [x[:, 0:PACK_W] for x in wu]
    u = [x[:, PACK_W:2 * PACK_W] for x in wu]

    while side:
        spacer()
    res1 = each(lambda a, b_, c: _dot(a.astype(BF16), jnp.concatenate([stk(b_), stk(c)], axis=1)), a_rb, u, w)
    res2 = each(lambda p, b_, c: _dot_tn(p["bh"], jnp.concatenate([b_, c], axis=1).astype(BF16)), pre, w, u)
    outs = []
    for p, x1, x2, avx, kv in zip(pre, res1, res2, av, khv):
        y0 = x1[:, 0:PACK_W] + avx[n:2 * n]
        rw = p["rt"] * p["g_ref"] + x1[:, PACK_W:2 * PACK_W]
        mm = jnp.where(bd, x2[:, 0:PACK_W], 0.0) + eye_w * p["g_tot"]
        cc = jnp.where(bd, x2[:, PACK_W:2 * PACK_W] + kv, 0.0)
        outs.append((y0, rw.astype(BF16), mm.astype(BF16), cc))
    return outs


def _rwkv_scan_kernel(r_ref, v_ref, nk_ref, kd_ref, b_ref, cl_ref, lw_ref, y_ref, y0_s, rw_s, mm_s, cc_s, g_s):
    d = pl.program_id(1)
    j = pl.program_id(2)
    fwd = d == 0

    @pl.when(j == 0)
    def _():
        g_s[...] = jnp.zeros(g_s.shape, F32)

    n = SCAN_CHUNK
    ngroups = BRANCH_W // PACK_W
    nchunks = r_ref.shape[1] // n
    t_idx = lax.broadcasted_iota(jnp.int32, (n, PACK_W), 0)
    s_idx = lax.broadcasted_iota(jnp.int32, (n, PACK_W), 1) & (n - 1)
    ahead = (t_idx - s_idx) * jnp.where(fwd, 1, -1)
    ms = (ahead > 0).astype(F32)
    mi = (ahead >= 0).astype(F32)
    eye = (ahead == 0).astype(F32)
    ri = lax.broadcasted_iota(jnp.int32, (PACK_W, PACK_W), 0)
    ci = lax.broadcasted_iota(jnp.int32, (PACK_W, PACK_W), 1)
    bd = (ri >> 6) == (ci >> 6)
    eye_w = (ri == ci).astype(F32)

    nquads = nchunks // SCAN_INTERLEAVE
    quad = lambda i: jnp.where(fwd, i, nquads - 1 - i)

    def run_local(q, side):
        insts, where = [], []
        for k in range(SCAN_INTERLEAVE):
            c = q * SCAN_INTERLEAVE + k
            rows = pl.ds(pl.multiple_of(c * n, n), n)
            for g in range(ngroups):
                cols = slice(g * PACK_W, (g + 1) * PACK_W)
                ld = lambda ref: ref[0, rows, cols].astype(F32)
                insts.append((ld(r_ref), kd_ref[0, 0, rows, cols].astype(F32), ld(v_ref), ld(nk_ref),
                              b_ref[0, 0, rows, cols].astype(F32), cl_ref[0, 0, rows, cols],
                              lw_ref[0, 0, rows, cols]))
                where.append((c, g, rows, cols))
        outs = _chunk_local(insts, ms, mi, eye, bd, eye_w, fwd, side)
        for (c, g, rows, cols), (y0, rw, mm, cc) in zip(where, outs):
            y0_s[rows, cols] = y0
            rw_s[rows, cols] = rw
            mm_s[c, g] = mm
            cc_s[c, g] = cc

    def chain_steps(q):
        state = [g_s[g] for g in range(ngroups)]

        def step(k):
            def run():
                c = q * SCAN_INTERLEAVE + jnp.where(fwd, k, SCAN_INTERLEAVE - 1 - k)
                rows = pl.ds(pl.multiple_of(c * n, n), n)
                for g in range(ngroups):
                    cols = slice(g * PACK_W, (g + 1) * PACK_W)
                    res = _dot(jnp.concatenate([rw_s[rows, cols], mm_s[c, g]], axis=0), state[g].astype(BF16))
                    y_ref[0, 0, rows, cols] = y0_s[rows, cols] + res[0:n]
                    state[g] = res[n:] + cc_s[c, g]
            return run

        def finish():
            for g in range(ngroups):
                g_s[g] = state[g]

        return [step(k) for k in range(SCAN_INTERLEAVE)], finish

    run_local(quad(0), ())

    def body(i, carry):
        steps, finish = chain_steps(quad(i - 1))
        run_local(quad(i), steps)
        finish()
        return carry

    lax.fori_loop(1, nquads, body, 0)
    steps, finish = chain_steps(quad(nquads - 1))
    for run in steps:
        run()
    finish()


def _rwkv_scan(prep, b, s):
    r, v, nk, kd, bb, cl, lw = prep
    ts = min(SCAN_TILE, s)
    nt = s // ts
    nchunks = ts // SCAN_CHUNK
    assert nchunks % SCAN_INTERLEAVE == 0
    ngroups = BRANCH_W // PACK_W
    tile = lambda d, j: j + d * (nt - 1 - 2 * j)
    tok = pl.BlockSpec((1, ts, BRANCH_W), lambda bi, d, j: (bi, tile(d, j), 0))
    per_dir = pl.BlockSpec((1, 1, ts, BRANCH_W), lambda bi, d, j: (d, bi, tile(d, j), 0))
    r3 = lambda a: a.reshape(b, s, BRANCH_W)
    r4 = lambda a: a.reshape(2, b, s, BRANCH_W)
    return pl.pallas_call(
        _rwkv_scan_kernel,
        grid=(b, 2, nt),
        in_specs=[tok, tok, tok, per_dir, per_dir, per_dir, per_dir],
        out_specs=per_dir,
        out_shape=jax.ShapeDtypeStruct((2, b, s, BRANCH_W), F32),
        scratch_shapes=[pltpu.VMEM((ts, BRANCH_W), F32), pltpu.VMEM((ts, BRANCH_W), BF16),
                        pltpu.VMEM((nchunks, ngroups, PACK_W, PACK_W), BF16),
                        pltpu.VMEM((nchunks, ngroups, PACK_W, PACK_W), F32),
                        pltpu.VMEM((ngroups, PACK_W, PACK_W), F32)],
        compiler_params=_params("parallel", "parallel", "arbitrary"),
        name="rwkv7_scan",
    )(r3(r), r3(v), r3(nk), r4(kd), r4(bb), r4(cl), r4(lw))


def _rwkv_finish_kernel(yf_ref, yb_ref, bonus_ref, gate_ref, lng_ref, lnb_ref, ones_ref, o_ref):
    ones_bd = ones_ref[...]
    y = yf_ref[0] + yb_ref[0]
    inv_hd = 1.0 / RWKV_HD
    ym = _head_sum(y, ones_bd) * inv_hd
    yc = y - ym
    yv = _head_sum(yc * yc, ones_bd) * inv_hd
    y = yc * lax.rsqrt(yv + LNX_EPS) * lng_ref[...] + lnb_ref[...] + bonus_ref[...]
    o_ref[...] = (y * gate_ref[...]).astype(BF16)


def _rwkv_finish(y_dirs, bonus, gate, wts, consts, *, seq):
    t = bonus.shape[0]
    ts = min(TOKEN_TILE, seq)
    row = lambda i: (i, 0)
    vec = lambda: _resident((1, BRANCH_W))
    tok = pl.BlockSpec((ts, BRANCH_W), row)
    return pl.pallas_call(
        _rwkv_finish_kernel,
        grid=(t // ts,),
        in_specs=[pl.BlockSpec((1, ts, BRANCH_W), lambda i: (0, i, 0)),
                  pl.BlockSpec((1, ts, BRANCH_W), lambda i: (1, i, 0)),
                  tok, tok, vec(), vec(), _resident((BRANCH_W, BRANCH_W))],
        out_specs=tok,
        out_shape=jax.ShapeDtypeStruct((t, BRANCH_W), BF16),
        compiler_params=_params("parallel"),
        name="rwkv7_finish",
    )(y_dirs, y_dirs, bonus, gate, wts["lnx_g"], wts["lnx_b"], consts["ones_bd"])


def _diff_attn_kernel(qt_ref, k_ref, vt_ref, lam_ref, ng_ref, o_ref, *, lam_init):
    tq = qt_ref.shape[3]
    s = k_ref.shape[1]
    qt = qt_ref[0, 0]
    row = lax.broadcasted_iota(jnp.int32, qt.shape, 0)
    zero = jnp.zeros((), qt.dtype)
    wq = jnp.concatenate([jnp.where(row < DIFF_HD, qt, zero), jnp.where(row >= DIFF_HD, qt, zero)], axis=1)
    pad = 2 * V7X_SUBLANES
    ones_rows = (lax.broadcasted_iota(jnp.int32, (pad, ATT_K_TILE), 0) == 0).astype(BF16)

    ntiles = s // ATT_K_TILE
    scores = lambda i: _dot(k_ref[0, i * ATT_K_TILE:(i + 1) * ATT_K_TILE, :], wq)
    m_old = jnp.full((1, 2 * tq), -1e30, F32)
    acc = jnp.zeros((DIFF_VD + pad, 2 * tq), F32)
    sc = scores(0)
    for i in range(ntiles):
        sc_next = scores(i + 1) if i + 1 < ntiles else None
        m_new = jnp.maximum(m_old, jnp.max(sc, axis=0, keepdims=True))
        p = jnp.exp2(sc - m_new).astype(BF16)
        vt_ext = jnp.concatenate([vt_ref[0, 0, :, i * ATT_K_TILE:(i + 1) * ATT_K_TILE], ones_rows], axis=0)
        acc = jnp.exp2(m_old - m_new) * acc + _dot(vt_ext, p)
        m_old, sc = m_new, sc_next

    lam = lam_ref[...]
    lam_full = (jnp.exp(jnp.sum(lam[0:1] * lam[1:2], keepdims=True))
                - jnp.exp(jnp.sum(lam[2:3] * lam[3:4], keepdims=True)) + lam_init)
    o = acc[0:DIFF_VD, :] / acc[DIFF_VD:DIFF_VD + 1, :]
    o = o[:, 0:tq] - lam_full * o[:, tq:2 * tq]
    o = o * lax.rsqrt(jnp.mean(o * o, axis=0, keepdims=True) + RMS_EPS) * ng_ref[...] * (1.0 - lam_init)
    o_ref[0] = o.T.astype(BF16)


def _diff_attention(q_t, k3, v_t, lam, norm_g, *, lam_init):
    b, h, _, s = q_t.shape
    tq = min(ATT_Q_TILE, s)
    return pl.pallas_call(
        functools.partial(_diff_attn_kernel, lam_init=lam_init),
        grid=(b, h, s // tq),
        in_specs=[pl.BlockSpec((1, 1, 2 * DIFF_HD, tq), lambda bi, hi, qi: (bi, hi, 0, qi)),
                  pl.BlockSpec((1, s, 2 * DIFF_HD), lambda bi, hi, qi: (bi, 0, hi)),
                  pl.BlockSpec((1, 1, DIFF_VD, s), lambda bi, hi, qi: (bi, hi, 0, 0)),
                  _resident((4, DIFF_HD)), _resident((DIFF_VD, 1))],
        out_specs=pl.BlockSpec((1, tq, DIFF_VD), lambda bi, hi, qi: (bi, qi, hi)),
        out_shape=jax.ShapeDtypeStruct((b, s, h * DIFF_VD), BF16),
        compiler_params=_params("parallel", "parallel", "arbitrary"),
        name="diff_attention",
    )(q_t, k3, v_t, lam, norm_g)


def _fnet_stage1_kernel(y_ref, m_ref, o_ref):
    n1 = y_ref.shape[1]
    c = BRANCH_W
    pq = _dot(m_ref[...], y_ref[0])
    for t2 in range(FNET_COL_TILE):
        re = slice(2 * c * t2, 2 * c * t2 + c)
        im = slice(2 * c * t2 + c, 2 * c * (t2 + 1))
        o_ref[0, :, re] = (pq[0:n1, re] + pq[n1:2 * n1, im]).astype(BF16)
        o_ref[0, :, im] = (pq[0:n1, im] - pq[n1:2 * n1, re]).astype(BF16)


def _fnet_stage2_kernel(a_ref, e_ref, o_ref, *, scale):
    c = BRANCH_W
    n2 = FNET_N2
    for f in range(FNET_F1_TILE):
        res = _dot(e_ref[f, :, 0:n2], a_ref[0, f, :, 0:c]) + _dot(e_ref[f, :, n2:2 * n2], a_ref[0, f, :, c:2 * c])
        o_ref[0, :, f * c:(f + 1) * c] = (res * scale).astype(BF16)


def _fourier_mixer(yf3, consts):
    b, s, c2 = yf3.shape
    n2 = FNET_N2
    n1 = s // n2
    width = FNET_COL_TILE * c2
    stage1 = pl.pallas_call(
        _fnet_stage1_kernel,
        grid=(b, n2 * c2 // width),
        in_specs=[pl.BlockSpec((1, n1, width), lambda bi, j: (bi, 0, j)),
                  _resident((2 * n1, n1))],
        out_specs=pl.BlockSpec((1, n1, width), lambda bi, j: (bi, 0, j)),
        out_shape=jax.ShapeDtypeStruct((b, n1, n2 * c2), BF16),
        compiler_params=_params("parallel", "parallel"),
        name="fnet_stage1",
    )(yf3.reshape(b, n1, n2 * c2), consts["dft1"][s])
    a4 = stage1.reshape(b, n1, n2, c2)
    f1t = FNET_F1_TILE
    out = pl.pallas_call(
        functools.partial(_fnet_stage2_kernel, scale=float(1.0 / math.sqrt(s * FNET_GD))),
        grid=(b, n1 // f1t),
        in_specs=[pl.BlockSpec((1, f1t, n2, c2), lambda bi, j: (bi, j, 0, 0)),
                  pl.BlockSpec((f1t, n2, 2 * n2), lambda bi, j: (j, 0, 0))],
        out_specs=pl.BlockSpec((1, n2, f1t * BRANCH_W), lambda bi, j: (bi, 0, j)),
        out_shape=jax.ShapeDtypeStruct((b, n2, n1 * BRANCH_W), BF16),
        compiler_params=_params("parallel", "parallel"),
        name="fnet_stage2",
    )(a4, consts["dft2"][s])
    return out.reshape(b, s, BRANCH_W)


def _pool_branch(p, prev8, next8, first, last, pos0, seq, scale, ext_ref):
    tm = p.shape[0]
    halo = V7X_SUBLANES
    ext_ref[0:halo, :] = prev8 * (1.0 - first.astype(F32))
    ext_ref[halo:halo + tm, :] = p
    ext_ref[halo + tm:2 * halo + tm, :] = next8 * (1.0 - last.astype(F32))
    pos = pos0 + lax.broadcasted_iota(jnp.int32, (tm, 1), 0)
    outs = []
    for g, win in enumerate(POOL_WINDOWS):
        hw = win // 2
        cols = slice(g * POOL_GD, (g + 1) * POOL_GD)
        acc = ext_ref[halo - hw:halo - hw + tm, cols]
        for off in range(-hw + 1, hw):
            acc = acc + ext_ref[halo + off:halo + off + tm, cols]
        count = (jnp.minimum(pos + hw, seq) - jnp.maximum(pos - hw, 0)).astype(F32)
        outs.append(acc / count - p[:, cols])
    return jnp.concatenate(outs, axis=1) * scale


def _merge_kernel(x_ref, mod_ref, yr_ref, yd_ref, yf_ref, pp_ref, ppp_ref, ppn_ref, gate_ref,
                  wb_ref, wo_ref, ps_ref, g_ref, b_ref, o_ref, ext_ref, *, tps, seq, alpha):
    i = pl.program_id(0)
    tm = x_ref.shape[0]
    it = i % tps
    y_pool = _pool_branch(pp_ref[...], ppp_ref[...], ppn_ref[...], it == 0, it == tps - 1,
                          it * tm, seq, ps_ref[...], ext_ref)
    branches = (yr_ref[...], yd_ref[...], yf_ref[...], y_pool.astype(BF16))
    merged = jnp.zeros((tm, D_MODEL), F32)
    for n, br in enumerate(branches):
        gate = gate_ref[:, n * D_MODEL:(n + 1) * D_MODEL].astype(F32)
        merged = merged + gate * _dot(br, wb_ref[n])
    upd = _dot(merged.astype(BF16), wo_ref[...])
    mod = mod_ref[0]
    y = alpha * x_ref[...] + (1.0 + mod[5:6]) * upd
    o_ref[...] = _layer_norm(y) * g_ref[...] + b_ref[...]


def _merge_sublayer(x2, mod, y_rwkv, y_diff, y_fnet, p_pool, gates, w_branch, w_out, pool_scale, ln_g, ln_b,
                    *, seq, alpha):
    t = x2.shape[0]
    tm = min(TOKEN_TILE, seq)
    tps = seq // tm
    r8 = tm // V7X_SUBLANES
    nblk8 = t // V7X_SUBLANES
    row = lambda i: (i, 0)
    br = lambda: pl.BlockSpec((tm, BRANCH_W), row)
    return pl.pallas_call(
        functools.partial(_merge_kernel, tps=tps, seq=seq, alpha=alpha),
        grid=(t // tm,),
        in_specs=[pl.BlockSpec((tm, D_MODEL), row),
                  pl.BlockSpec((1, 9, D_MODEL), lambda i: (i // tps, 0, 0)),
                  br(), br(), br(), br(),
                  pl.BlockSpec((V7X_SUBLANES, BRANCH_W), lambda i: (jnp.maximum(i * r8 - 1, 0), 0)),
                  pl.BlockSpec((V7X_SUBLANES, BRANCH_W), lambda i: (jnp.minimum((i + 1) * r8, nblk8 - 1), 0)),
                  pl.BlockSpec((tm, N_BRANCH * D_MODEL), row),
                  _resident((N_BRANCH, BRANCH_W, D_MODEL)), _resident((D_MODEL, D_MODEL)),
                  _resident((1, BRANCH_W)), _resident((1, D_MODEL)), _resident((1, D_MODEL))],
        out_specs=pl.BlockSpec((tm, D_MODEL), row),
        out_shape=jax.ShapeDtypeStruct((t, D_MODEL), F32),
        scratch_shapes=[pltpu.VMEM((tm + 2 * V7X_SUBLANES, BRANCH_W), F32)],
        compiler_params=_params("parallel"),
        name="merge_sublayer",
    )(x2, mod, y_rwkv, y_diff, y_fnet, p_pool, p_pool, p_pool, gates, w_branch, w_out, pool_scale, ln_g, ln_b)


def _rope_tables(s):
    inv = ROPE_THETA ** (-jnp.arange(0, ROPE_DIM, 2, dtype=F32) / ROPE_DIM)
    ang = jnp.arange(s, dtype=F32)[:, None] * inv[None, :]
    cos, sin = jnp.cos(ang), jnp.sin(ang)
    half = ROPE_DIM // 2
    pad = jnp.zeros((s, DIFF_HD - ROPE_DIM), F32)
    zero = jnp.zeros((s, half), F32)
    c64 = jnp.concatenate([cos, cos, pad + 1.0], axis=1)
    s1_64 = jnp.concatenate([-sin, zero, pad], axis=1)
    s2_64 = jnp.concatenate([zero, sin, pad], axis=1)
    return tuple(jnp.concatenate([x, x], axis=1) for x in (c64, s1_64, s2_64))


def _group_dft_matrix():
    c = np.arange(FNET_GD)
    ang = 2.0 * np.pi * ((c[:, None] * c[None, :]) % FNET_GD) / FNET_GD
    eye = np.eye(FNET_GROUPS)
    return jnp.asarray(np.concatenate([np.kron(eye, np.cos(ang)), -np.kron(eye, np.sin(ang))], axis=1), BF16)


def _seq_dft_tables(s):
    n2 = FNET_N2
    n1 = s // n2
    a = np.arange(n1)
    ang1 = 2.0 * np.pi * ((a[:, None] * a[None, :]) % n1) / n1
    dft1 = np.concatenate([np.cos(ang1), np.sin(ang1)], axis=0)
    f = a[:, None, None] + n1 * np.arange(n2)[None, :, None]
    ang2 = 2.0 * np.pi * ((f * np.arange(n2)[None, None, :]) % s) / s
    dft2 = np.concatenate([np.cos(ang2), np.sin(ang2)], axis=2)
    return jnp.asarray(dft1, BF16), jnp.asarray(dft2, BF16)


def _scan_cumsum_matrices(ts):
    t = np.arange(ts)
    same = (t[:, None] // SCAN_CHUNK) == (t[None, :] // SCAN_CHUNK)
    fwd = same & (t[None, :] <= t[:, None])
    bwd = same & (t[None, :] >= t[:, None])
    return jnp.asarray(np.stack([fwd, bwd]).astype(np.float32), BF16)


def _constants(seqs):
    h = np.arange(BRANCH_W) // RWKV_HD
    consts = {
        "ones_bd": jnp.asarray((h[:, None] == h[None, :]).astype(np.float32), BF16),
        "fmat": _group_dft_matrix(),
        "rope": {}, "dft1": {}, "dft2": {}, "tri": {},
    }
    for s in seqs:
        consts["rope"][s] = _rope_tables(s)
        consts["dft1"][s], consts["dft2"][s] = _seq_dft_tables(s)
        ts = min(TOKEN_TILE, s)
        consts["tri"][ts] = _scan_cumsum_matrices(ts)
    return consts


def _pad_lora(w2):
    z = jnp.zeros_like(w2[0])
    return jnp.stack([jnp.concatenate([w2[0], z], axis=0), jnp.concatenate([z, w2[1]], axis=0)]).astype(BF16)


def _encoder_layer(x2, b, s, mod, lam_init, w, consts):
    alpha = (2.0 * DEPTH) ** 0.25
    x2 = _ffn_sublayer(x2, mod, w["ln_g"][0], w["ln_b"][0], w["ffa_in"], w["ffa_out"], sub=0, seq=s, alpha=alpha)
    p_rwkv, q_t, k_tok, v_t, p_fnet, p_pool, gates = _mixer_projection(x2, mod, w["w_in"], consts["rope"][s],
                                                               consts["fmat"], seq=s)
    prep = _rwkv_prep(p_rwkv, w["rwkv"], consts, seq=s)
    y_dirs = _rwkv_scan(prep[:7], b, s)
    y_rwkv = _rwkv_finish(y_dirs.reshape(2, b * s, BRANCH_W), prep[7], prep[8], w["rwkv"], consts, seq=s)
    y_diff = _diff_attention(q_t, k_tok.reshape(b, s, BRANCH_W), v_t, w["diff_lam"], w["diff_norm_g"],
                             lam_init=lam_init)
    y_fnet = _fourier_mixer(p_fnet.reshape(b, s, 2 * BRANCH_W), consts)
    x2 = _merge_sublayer(x2, mod, y_rwkv, y_diff.reshape(b * s, BRANCH_W), y_fnet.reshape(b * s, BRANCH_W),
                         p_pool, gates, w["w_branch"], w["w_out"], w["pool_scale"], w["ln_g"][1], w["ln_b"][1],
                         seq=s, alpha=alpha)
    x2 = _ffn_sublayer(x2, mod, w["ln_g"][2], w["ln_b"][2], w["ffb_in"], w["ffb_out"], sub=2, seq=s, alpha=alpha)
    return x2


def kernel(x_prompt, x_sample, c_prompt, c_sample, ada_w, ada_b, ln_g, ln_b, ffa_w_in, ffa_w_out, ffb_w_in, ffb_w_out, w_in, rwkv_mu, rwkv_w0, rwkv_w2, rwkv_a0, rwkv_a2, rwkv_g2, rwkv_kk, rwkv_ka, rwkv_rk, rwkv_lnx_g, rwkv_lnx_b, diff_lam, diff_norm_g, pool_scale, w_branch, w_out):
    groups = ((x_prompt, c_prompt), (x_sample, c_sample))
    consts = _constants(sorted({x.shape[1] for x, _ in groups}))
    depth = ada_w.shape[0]

    nb = [c.shape[0] for _, c in groups]
    c_all = jnp.concatenate([c for _, c in groups], axis=0)
    rows = -(-c_all.shape[0] // V7X_SUBLANES) * V7X_SUBLANES
    c_pad = jnp.pad(c_all, ((0, rows - c_all.shape[0]), (0, 0)))

    layers = []
    for l in range(depth):
        mod_all = _modulation(c_pad, ada_w[l], ada_b[l]).reshape(rows, 9, D_MODEL)
        vec = lambda a: a[l].reshape(1, -1)
        layers.append({
            "mods": (mod_all[:nb[0]], mod_all[nb[0]:nb[0] + nb[1]]),
            "lam_init": 0.8 - 0.6 * math.exp(-0.3 * l),
            "ln_g": [ln_g[l, j].reshape(1, -1) for j in range(3)],
            "ln_b": [ln_b[l, j].reshape(1, -1) for j in range(3)],
            "ffa_in": ffa_w_in[l].astype(BF16), "ffa_out": ffa_w_out[l].astype(BF16),
            "ffb_in": ffb_w_in[l].astype(BF16), "ffb_out": ffb_w_out[l].astype(BF16),
            "w_in": w_in[l].astype(BF16),
            "w_branch": w_branch[l].astype(BF16), "w_out": w_out[l].astype(BF16),
            "pool_scale": vec(pool_scale),
            "diff_lam": diff_lam[l], "diff_norm_g": diff_norm_g[l].reshape(-1, 1),
            "rwkv": {
                "mu": vec(rwkv_mu), "kk": vec(rwkv_kk), "ka": vec(rwkv_ka), "rk": vec(rwkv_rk),
                "lnx_g": vec(rwkv_lnx_g), "lnx_b": vec(rwkv_lnx_b),
                "w0": rwkv_w0[l].reshape(2, 1, BRANCH_W), "a0": rwkv_a0[l].reshape(2, 1, BRANCH_W),
                "w2pad": _pad_lora(rwkv_w2[l]), "a2pad": _pad_lora(rwkv_a2[l]),
                "g2": rwkv_g2[l].astype(BF16),
            },
        })

    outs = []
    for gi, (x, _) in enumerate(groups):
        b, s, d = x.shape
        x2 = x.reshape(b * s, d)
        for w in layers:
            x2 = _encoder_layer(x2, b, s, w["mods"][gi], w["lam_init"], w, consts)
        outs.append(x2.reshape(b, s, d))
    return tuple(outs)
```

```python
import functools
import math

import numpy as np
import jax
import jax.numpy as jnp
from jax import lax
from jax.experimental import pallas as pl
from jax.experimental.pallas import tpu as pltpu

D_MODEL = 1024
DEPTH = 2
BRANCH_W = 512
N_BRANCH = 4
RWKV_HEADS = 8
RWKV_HD = 64
W_LORA = 64
A_LORA = 64
G_LORA = 128
DIFF_HEADS = 4
DIFF_HD = 64
DIFF_VD = 128
ROPE_DIM = DIFF_HD // 4
ROPE_THETA = 500000.0
FNET_GROUPS = 4
FNET_GD = BRANCH_W // FNET_GROUPS
POOL_GROUPS = 4
POOL_GD = BRANCH_W // POOL_GROUPS
POOL_WINDOWS = (2, 4, 8, 16)
D_FF = 2816
LN_EPS = 1e-5
LNX_EPS = 64e-5
RMS_EPS = 1e-5

RWKV_COLS = 3 * BRANCH_W + 2 * W_LORA + 2 * A_LORA + G_LORA
DIFF_COLS = 2 * DIFF_HEADS * 2 * DIFF_HD + DIFF_HEADS * DIFF_VD
OFF_DIFF = RWKV_COLS
OFF_FNET = OFF_DIFF + DIFF_COLS
OFF_POOL = OFF_FNET + BRANCH_W
OFF_GATE = OFF_POOL + BRANCH_W
N_IN = OFF_GATE + N_BRANCH * D_MODEL
OFF_W1 = 3 * BRANCH_W
OFF_A1 = OFF_W1 + 2 * W_LORA
OFF_G1 = OFF_A1 + 2 * A_LORA

V7X_SUBLANES = 8
V7X_LANES = 128
V7X_VMEM_LIMIT_BYTES = 56 * 1024 * 1024

TOKEN_TILE = 512
FF_CHUNK = 256
SCAN_TILE = 1024
SCAN_CHUNK = 64
SCAN_INTERLEAVE = 4
HEAD_PACK = 4
PACK_W = HEAD_PACK * RWKV_HD
ATT_Q_TILE = 512
ATT_K_TILE = 512
FNET_N2 = 128
FNET_F1_TILE = 8
FNET_COL_TILE = 8

LOG2_E = 1.4426950408889634

BF16 = jnp.bfloat16
F32 = jnp.float32


def _dot(a, b):
    return jnp.dot(a, b, preferred_element_type=F32)


def _dot_nt(a, b):
    return lax.dot_general(a, b, (((1,), (1,)), ((), ())), preferred_element_type=F32)


def _dot_tn(a, b):
    return lax.dot_general(a, b, (((0,), (0,)), ((), ())), preferred_element_type=F32)


def _split_bf16(x):
    hi = x.astype(BF16)
    lo = (x - hi.astype(F32)).astype(BF16)
    return hi, lo


def _layer_norm(x, eps=LN_EPS):
    mu = jnp.mean(x, axis=-1, keepdims=True)
    xc = x - mu
    var = jnp.mean(xc * xc, axis=-1, keepdims=True)
    return xc * lax.rsqrt(var + eps)


def _params(*semantics):
    return pltpu.CompilerParams(dimension_semantics=semantics, vmem_limit_bytes=V7X_VMEM_LIMIT_BYTES)


def _resident(shape):
    nd = len(shape)
    return pl.BlockSpec(shape, lambda *_: (0,) * nd, pipeline_mode=pl.Buffered(1))


def _mod_kernel(c_ref, w_ref, b_ref, o_ref):
    c = c_ref[...]
    a = c * jax.nn.sigmoid(c)
    a_hi, a_lo = _split_bf16(a)
    w_hi, w_lo = _split_bf16(w_ref[...])
    o_ref[...] = _dot(a_hi, w_hi) + _dot(a_lo, w_hi) + _dot(a_hi, w_lo) + b_ref[...]


def _modulation(c_pad, ada_w, ada_b):
    rows = c_pad.shape[0]
    n = ada_w.shape[1]
    nb = 9 * V7X_LANES
    return pl.pallas_call(
        _mod_kernel,
        grid=(n // nb,),
        in_specs=[pl.BlockSpec((rows, D_MODEL), lambda i: (0, 0)),
                  pl.BlockSpec((D_MODEL, nb), lambda i: (0, i)),
                  pl.BlockSpec((1, nb), lambda i: (0, i))],
        out_specs=pl.BlockSpec((rows, nb), lambda i: (0, i)),
        out_shape=jax.ShapeDtypeStruct((rows, n), F32),
        compiler_params=_params("arbitrary"),
        name="adaln_mod",
    )(c_pad, ada_w, ada_b.reshape(1, n))


def _ffn_kernel(x_ref, mod_ref, g_ref, b_ref, win_ref, wout_ref, o_ref, *, sub, alpha):
    x = x_ref[...]
    mod = mod_ref[0]
    shift, scale, gate = mod[3 * sub:3 * sub + 1], mod[3 * sub + 1:3 * sub + 2], mod[3 * sub + 2:3 * sub + 3]
    h = (_layer_norm(x) * (1.0 + scale) + shift).astype(BF16)
    acc = jnp.zeros(x.shape, F32)
    for c in range(D_FF // FF_CHUNK):
        lo = c * FF_CHUNK
        ua = _dot(h, win_ref[:, lo:lo + FF_CHUNK])
        ub = _dot(h, win_ref[:, D_FF + lo:D_FF + lo + FF_CHUNK])
        act = (ua * jax.nn.sigmoid(ua) * ub).astype(BF16)
        acc = acc + _dot(act, wout_ref[lo:lo + FF_CHUNK, :])
    y = alpha * x + (1.0 + gate) * (0.5 * acc)
    o_ref[...] = _layer_norm(y) * g_ref[...] + b_ref[...]


def _ffn_sublayer(x2, mod, ln_g, ln_b, w_in, w_out, *, sub, seq, alpha):
    t = x2.shape[0]
    tm = min(TOKEN_TILE, seq)
    tps = seq // tm
    return pl.pallas_call(
        functools.partial(_ffn_kernel, sub=sub, alpha=alpha),
        grid=(t // tm,),
        in_specs=[pl.BlockSpec((tm, D_MODEL), lambda i: (i, 0)),
                  pl.BlockSpec((1, 9, D_MODEL), lambda i: (i // tps, 0, 0)),
                  _resident((1, D_MODEL)), _resident((1, D_MODEL)),
                  _resident((D_MODEL, 2 * D_FF)), _resident((D_FF, D_MODEL))],
        out_specs=pl.BlockSpec((tm, D_MODEL), lambda i: (i, 0)),
        out_shape=jax.ShapeDtypeStruct((t, D_MODEL), F32),
        compiler_params=_params("parallel"),
        name=f"ffn_sublayer{sub}",
    )(x2, mod, ln_g, ln_b, w_in, w_out)


def _proj_kernel(x_ref, mod_ref, w_ref, rc_ref, rs1_ref, rs2_ref, fmat_ref,
                 o_rwkv, o_qt, o_k, o_vt, o_fnet, o_pool, o_gate):
    x = x_ref[...]
    mod = mod_ref[0]
    h = (_layer_norm(x) * (1.0 + mod[4:5]) + mod[3:4]).astype(BF16)

    for lo in range(0, RWKV_COLS, 640):
        o_rwkv[:, lo:lo + 640] = _dot(h, w_ref[:, lo:lo + 640])

    rc = jnp.concatenate([rc_ref[...]] * 4, axis=1)
    rs1 = jnp.concatenate([rs1_ref[...]] * 4, axis=1)
    rs2 = jnp.concatenate([rs2_ref[...]] * 4, axis=1)
    hw = 2 * DIFF_HD
    for blk, scl in ((0, DIFF_HD ** -0.5 * LOG2_E), (1, 1.0)):
        lo = OFF_DIFF + blk * 512
        u = _dot(h, w_ref[:, lo:lo + 512])
        rot = u * rc + pltpu.roll(u, 512 - ROPE_DIM // 2, 1) * rs1 + pltpu.roll(u, ROPE_DIM // 2, 1) * rs2
        if blk == 0:
            rot = rot * scl
            for hd in range(DIFF_HEADS):
                o_qt[0, hd] = rot[:, hd * hw:(hd + 1) * hw].T.astype(BF16)
        else:
            o_k[...] = rot.astype(BF16)
    lo = OFF_DIFF + 1024
    vv = _dot(h, w_ref[:, lo:lo + 512])
    for hd in range(DIFF_HEADS):
        o_vt[0, hd] = vv[:, hd * DIFF_VD:(hd + 1) * DIFF_VD].T.astype(BF16)

    pf = _dot(h, w_ref[:, OFF_FNET:OFF_FNET + BRANCH_W]).astype(BF16)
    o_fnet[...] = _dot(pf, fmat_ref[...]).astype(BF16)

    o_pool[...] = _dot(h, w_ref[:, OFF_POOL:OFF_POOL + BRANCH_W])

    for c in range(N_BRANCH * D_MODEL // 512):
        lo = OFF_GATE + c * 512
        o_gate[:, c * 512:(c + 1) * 512] = jax.nn.sigmoid(_dot(h, w_ref[:, lo:lo + 512])).astype(BF16)


def _mixer_projection(x2, mod, w_in, rope, fmat, *, seq):
    t = x2.shape[0]
    tm = min(TOKEN_TILE, seq)
    tps = seq // tm
    row = lambda i: (i, 0)
    rope_spec = pl.BlockSpec((tm, V7X_LANES), lambda i: (i % tps, 0))
    nbatch = t // seq
    head_t = pl.BlockSpec((1, DIFF_HEADS, 2 * DIFF_HD, tm), lambda i: (i // tps, 0, 0, i % tps))
    head_t_shape = jax.ShapeDtypeStruct((nbatch, DIFF_HEADS, 2 * DIFF_HD, seq), BF16)
    return pl.pallas_call(
        _proj_kernel,
        grid=(t // tm,),
        in_specs=[pl.BlockSpec((tm, D_MODEL), row),
                  pl.BlockSpec((1, 9, D_MODEL), lambda i: (i // tps, 0, 0)),
                  _resident((D_MODEL, N_IN)),
                  rope_spec, rope_spec, rope_spec,
                  _resident((BRANCH_W, 2 * BRANCH_W))],
        out_specs=[pl.BlockSpec((tm, RWKV_COLS), row),
                   head_t, pl.BlockSpec((tm, BRANCH_W), row), head_t,
                   pl.BlockSpec((tm, 2 * BRANCH_W), row),
                   pl.BlockSpec((tm, BRANCH_W), row),
                   pl.BlockSpec((tm, N_BRANCH * D_MODEL), row)],
        out_shape=[jax.ShapeDtypeStruct((t, RWKV_COLS), F32),
                   head_t_shape, jax.ShapeDtypeStruct((t, BRANCH_W), BF16), head_t_shape,
                   jax.ShapeDtypeStruct((t, 2 * BRANCH_W), BF16),
                   jax.ShapeDtypeStruct((t, BRANCH_W), F32),
                   jax.ShapeDtypeStruct((t, N_BRANCH * D_MODEL), BF16)],
        compiler_params=_params("parallel"),
        name="mixer_projection",
    )(x2, mod, w_in, rope[0], rope[1], rope[2], fmat)


def _token_shift(p, prev8, next8, first, last, mu):
    n = p.shape[0]
    prev_row = prev8[V7X_SUBLANES - 1:V7X_SUBLANES, :] * (1.0 - first.astype(F32))
    next_row = next8[0:1, :] * (1.0 - last.astype(F32))
    row = lax.broadcasted_iota(jnp.int32, p.shape, 0)
    up = jnp.where(row == 0, prev_row, pltpu.roll(p, 1, 0))
    dn = jnp.where(row == n - 1, next_row, pltpu.roll(p, n - 1, 0))
    return p + (0.5 * (up + dn) - p) * mu


def _head_sum(x, ones_bd):
    return _dot(x.astype(BF16), ones_bd)


def _rwkv_common(ps, kkw, ones_bd):
    r = ps[:, 0:BRANCH_W]
    k = ps[:, BRANCH_W:2 * BRANCH_W]
    v = ps[:, 2 * BRANCH_W:3 * BRANCH_W]
    kk = k * kkw
    kk = kk / jnp.maximum(jnp.sqrt(_head_sum(kk * kk, ones_bd)), 1e-12)
    return r, k, v, kk


def _rwkv_direction(ps, k, w0, w2pad, a0, a2pad, ka):
    z = w0 + _dot(jnp.tanh(ps[:, OFF_W1:OFF_W1 + 2 * W_LORA]).astype(BF16), w2pad)
    w_log = -(jnp.maximum(-z, 0.0) + jnp.log(1.0 + jnp.exp(-jnp.abs(z)))) - 0.5
    logw = -jnp.exp(w_log)
    a = jax.nn.sigmoid(a0 + _dot(ps[:, OFF_A1:OFF_A1 + 2 * A_LORA].astype(BF16), a2pad))
    kd = k * (1.0 + (a - 1.0) * ka)
    return logw, a, kd


def _rwkv_prep_kernel(p_ref, pp_ref, pn_ref, mu_ref, kkw_ref, ka_ref, rk_ref, w0_ref, w2_ref, a0_ref, a2_ref,
                      g2_ref, ones_ref, tri_ref, r_o, v_o, nk_o, kd_o, b_o, cl_o, lw_o, bonus_o, gate_o, *, tps):
    i = pl.program_id(0)
    first = (i % tps) == 0
    last = (i % tps) == tps - 1
    ones_bd = ones_ref[...]
    ps = _token_shift(p_ref[...], pp_ref[...], pn_ref[...], first, last, mu_ref[...])
    r, k, v, kk = _rwkv_common(ps, kkw_ref[...], ones_bd)
    r_o[...] = r.astype(BF16)
    v_o[...] = v.astype(BF16)
    nk_o[...] = (-kk).astype(BF16)
    k_sum = jnp.zeros_like(k)
    for d in range(2):
        logw, a, kd = _rwkv_direction(ps, k, w0_ref[d], w2_ref[d], a0_ref[d], a2_ref[d], ka_ref[...])
        kd_o[d] = kd.astype(BF16)
        b_o[d] = (kk * a).astype(BF16)
        lw_o[d] = logw.astype(BF16)
        lw_hi, lw_lo = _split_bf16(logw)
        cl_o[d] = _dot(tri_ref[d], lw_hi) + _dot(tri_ref[d], lw_lo)
        k_sum = k_sum + kd
    bonus_o[...] = (_head_sum(r * (0.5 * k_sum) * rk_ref[...], ones_bd) * v).astype(BF16)
    gate_o[...] = _dot(jax.nn.sigmoid(ps[:, OFF_G1:OFF_G1 + G_LORA]).astype(BF16), g2_ref[...]).astype(BF16)


def _rwkv_prep(p2, wts, consts, *, seq):
    t = p2.shape[0]
    ts = min(TOKEN_TILE, seq)
    tps = seq // ts
    r8 = ts // V7X_SUBLANES
    nblk8 = t // V7X_SUBLANES
    row = lambda i: (i, 0)
    vec = lambda: _resident((1, BRANCH_W))
    tok = lambda dt: jax.ShapeDtypeStruct((t, BRANCH_W), dt)
    both = lambda dt: jax.ShapeDtypeStruct((2, t, BRANCH_W), dt)
    tok_spec = pl.BlockSpec((ts, BRANCH_W), row)
    both_spec = pl.BlockSpec((2, ts, BRANCH_W), lambda i: (0, i, 0))
    return pl.pallas_call(
        functools.partial(_rwkv_prep_kernel, tps=tps),
        grid=(t // ts,),
        in_specs=[pl.BlockSpec((ts, RWKV_COLS), row),
                  pl.BlockSpec((V7X_SUBLANES, RWKV_COLS), lambda i: (jnp.maximum(i * r8 - 1, 0), 0)),
                  pl.BlockSpec((V7X_SUBLANES, RWKV_COLS), lambda i: (jnp.minimum((i + 1) * r8, nblk8 - 1), 0)),
                  _resident((1, RWKV_COLS)), vec(), vec(), vec(),
                  _resident((2, 1, BRANCH_W)), _resident((2, 2 * W_LORA, BRANCH_W)),
                  _resident((2, 1, BRANCH_W)), _resident((2, 2 * A_LORA, BRANCH_W)),
                  _resident((G_LORA, BRANCH_W)), _resident((BRANCH_W, BRANCH_W)), _resident((2, ts, ts))],
        out_specs=[tok_spec, tok_spec, tok_spec, both_spec, both_spec, both_spec, both_spec, tok_spec, tok_spec],
        out_shape=[tok(BF16), tok(BF16), tok(BF16), both(BF16), both(BF16), both(F32), both(BF16), tok(BF16), tok(BF16)],
        compiler_params=_params("parallel"),
        name="rwkv7_prep",
    )(p2, p2, p2, wts["mu"], wts["kk"], wts["ka"], wts["rk"], wts["w0"], wts["w2pad"], wts["a0"], wts["a2pad"],
      wts["g2"], consts["ones_bd"], consts["tri"][ts])


def _stack_heads(y, bd):
    return jnp.where(bd, jnp.concatenate([y] * HEAD_PACK, axis=0), jnp.zeros((), y.dtype))


def _chunk_local(insts, ms, mi, eye, bd, eye_w, fwd, side=()):
    side = list(side)
    spacer = lambda: side.pop(0)() if side else None
    n = SCAN_CHUNK
    half = n // 2
    stk = lambda y: _stack_heads(y.astype(BF16), bd)
    each = lambda f, *lists: [f(*xs) for xs in zip(*lists)]

    pre = []
    for r, kd, v, nk, b, cl, lw in insts:
        ref = cl[half:half + 1, :]
        tot = jnp.where(fwd, cl[n - 1:n, :], cl[0:1, :])
        e_i = jnp.exp(ref - cl)
        e_e = jnp.exp(tot - cl)
        g_ref = jnp.exp(ref)
        rt = r * jnp.exp(cl - ref)
        at = nk * jnp.exp(cl - lw - ref)
        pre.append(dict(rt=rt, at=at, bt=(b * e_i).astype(BF16), kt=(kd * e_i).astype(BF16),
                        bh=(b * e_e).astype(BF16), kh=(kd * e_e).astype(BF16), vb=v.astype(BF16),
                        g_ref=g_ref, g_tot=jnp.exp(tot)))

    prod = [_dot_nt(jnp.concatenate([p["at"], p["rt"]], axis=0).astype(BF16),
                    jnp.concatenate([stk(p["bt"]), stk(p["kt"])], axis=0)) for p in pre]
    spacer()
    a_ab = [x[0:n, 0:PACK_W] * ms for x in prod]
    a_rb = [x[n:2 * n, 0:PACK_W] * mi for x in prod]
    a_kk = [jnp.concatenate([x[0:n, PACK_W:2 * PACK_W] * ms, x[n:2 * n, PACK_W:2 * PACK_W] * mi], axis=0)
            for x in prod]
    av = [_dot(a.astype(BF16), stk(p["vb"])) for a, p in zip(a_kk, pre)]

    steps = int(math.log2(n))
    pw = each(lambda a: _dot(a.astype(BF16), stk(a)), a_ab)
    tinv = [eye + a for a in a_ab]
    for k in range(1, steps):
        if k % 2 == 1:
            spacer()
        if k < steps - 1:
            res = each(lambda a, t: _dot(a.astype(BF16), jnp.concatenate([stk(a), stk(t)], axis=1)), pw, tinv)
            pw = [x[:, 0:PACK_W] for x in res]
            tinv = each(lambda t, x: t + x[:, PACK_W:2 * PACK_W], tinv, res)
        else:
            tinv = each(lambda a, t: t + _dot(a.astype(BF16), stk(t)), pw, tinv)
    wu = each(lambda t, p, x: _dot(t.astype(BF16), jnp.concatenate([stk(p["at"] * p["g_ref"]), stk(x[0:n])], axis=1)),
              tinv, pre, av)
    w = [x[:, 0:PACK_W] for x in wu]
    u = [x[:, PACK_W:2 * PACK_W] for x in wu]

    while side:
        spacer()
    res1 = each(lambda a, b_, c: _dot(a.astype(BF16), jnp.concatenate([stk(b_), stk(c)], axis=1)), a_rb, u, w)
    res2 = each(lambda p, b_, c: _dot_tn(
        jnp.concatenate([p["bh"], p["kh"]], axis=0),
        jnp.concatenate([jnp.concatenate([b_, c], axis=1).astype(BF16),
                         jnp.concatenate([jnp.zeros_like(p["vb"]), p["vb"]], axis=1)], axis=0)), pre, w, u)
    outs = []
    for p, x1, x2, avx in zip(pre, res1, res2, av):
        y0 = x1[:, 0:PACK_W] + avx[n:2 * n]
        rw = p["rt"] * p["g_ref"] + x1[:, PACK_W:2 * PACK_W]
        mm = jnp.where(bd, x2[:, 0:PACK_W], 0.0) + eye_w * p["g_tot"]
        cc = jnp.where(bd, x2[:, PACK_W:2 * PACK_W], 0.0)
        outs.append((y0, rw.astype(BF16), mm.astype(BF16), cc))
    return outs


def _rwkv_scan_kernel(r_ref, v_ref, nk_ref, kd_ref, b_ref, cl_ref, lw_ref, y_ref, y0_s, rw_s, mm_s, cc_s, g_s):
    d = pl.program_id(1)
    j = pl.program_id(2)
    fwd = d == 0

    @pl.when(j == 0)
    def _():
        g_s[...] = jnp.zeros(g_s.shape, F32)

    n = SCAN_CHUNK
    ngroups = BRANCH_W // PACK_W
    nchunks = r_ref.shape[1] // n
    t_idx = lax.broadcasted_iota(jnp.int32, (n, PACK_W), 0)
    s_idx = lax.broadcasted_iota(jnp.int32, (n, PACK_W), 1) & (n - 1)
    ahead = (t_idx - s_idx) * jnp.where(fwd, 1, -1)
    ms = (ahead > 0).astype(F32)
    mi = (ahead >= 0).astype(F32)
    eye = (ahead == 0).astype(F32)
    ri = lax.broadcasted_iota(jnp.int32, (PACK_W, PACK_W), 0)
    ci = lax.broadcasted_iota(jnp.int32, (PACK_W, PACK_W), 1)
    bd = (ri >> 6) == (ci >> 6)
    eye_w = (ri == ci).astype(F32)

    nquads = nchunks // SCAN_INTERLEAVE
    quad = lambda i: jnp.where(fwd, i, nquads - 1 - i)

    def run_local(q, side):
        insts, where = [], []
        for k in range(SCAN_INTERLEAVE):
            c = q * SCAN_INTERLEAVE + k
            rows = pl.ds(pl.multiple_of(c * n, n), n)
            for g in range(ngroups):
                cols = slice(g * PACK_W, (g + 1) * PACK_W)
                ld = lambda ref: ref[0, rows, cols].astype(F32)
                insts.append((ld(r_ref), kd_ref[0, 0, rows, cols].astype(F32), ld(v_ref), ld(nk_ref),
                              b_ref[0, 0, rows, cols].astype(F32), cl_ref[0, 0, rows, cols],
                              lw_ref[0, 0, rows, cols].astype(F32)))
                where.append((c, g, rows, cols))
        outs = _chunk_local(insts, ms, mi, eye, bd, eye_w, fwd, side)
        for (c, g, rows, cols), (y0, rw, mm, cc) in zip(where, outs):
            y0_s[rows, cols] = y0
            rw_s[rows, cols] = rw
            mm_s[c, g] = mm
            cc_s[c, g] = cc

    def chain_steps(q):
        state = [g_s[g] for g in range(ngroups)]

        def step(k):
            def run():
                c = q * SCAN_INTERLEAVE + jnp.where(fwd, k, SCAN_INTERLEAVE - 1 - k)
                rows = pl.ds(pl.multiple_of(c * n, n), n)
                for g in range(ngroups):
                    cols = slice(g * PACK_W, (g + 1) * PACK_W)
                    res = _dot(jnp.concatenate([rw_s[rows, cols], mm_s[c, g]], axis=0), state[g].astype(BF16))
                    y_ref[0, 0, rows, cols] = y0_s[rows, cols] + res[0:n]
                    state[g] = res[n:] + cc_s[c, g]
            return run

        def finish():
            for g in range(ngroups):
                g_s[g] = state[g]

        return [step(k) for k in range(SCAN_INTERLEAVE)], finish

    run_local(quad(0), ())

    def body(i, carry):
        steps, finish = chain_steps(quad(i - 1))
        run_local(quad(i), steps)
        finish()
        return carry

    lax.fori_loop(1, nquads, body, 0)
    steps, finish = chain_steps(quad(nquads - 1))
    for run in steps:
        run()
    finish()


def _rwkv_scan(prep, b, s):
    r, v, nk, kd, bb, cl, lw = prep
    ts = min(SCAN_TILE, s)
    nt = s // ts
    nchunks = ts // SCAN_CHUNK
    assert nchunks % SCAN_INTERLEAVE == 0
    ngroups = BRANCH_W // PACK_W
    tile = lambda d, j: j + d * (nt - 1 - 2 * j)
    tok = pl.BlockSpec((1, ts, BRANCH_W), lambda bi, d, j: (bi, tile(d, j), 0))
    per_dir = pl.BlockSpec((1, 1, ts, BRANCH_W), lambda bi, d, j: (d, bi, tile(d, j), 0))
    r3 = lambda a: a.reshape(b, s, BRANCH_W)
    r4 = lambda a: a.reshape(2, b, s, BRANCH_W)
    return pl.pallas_call(
        _rwkv_scan_kernel,
        grid=(b, 2, nt),
        in_specs=[tok, tok, tok, per_dir, per_dir, per_dir, per_dir],
        out_specs=per_dir,
        out_shape=jax.ShapeDtypeStruct((2, b, s, BRANCH_W), F32),
        scratch_shapes=[pltpu.VMEM((ts, BRANCH_W), F32), pltpu.VMEM((ts, BRANCH_W), BF16),
                        pltpu.VMEM((nchunks, ngroups, PACK_W, PACK_W), BF16),
                        pltpu.VMEM((nchunks, ngroups, PACK_W, PACK_W), F32),
                        pltpu.VMEM((ngroups, PACK_W, PACK_W), F32)],
        compiler_params=_params("parallel", "parallel", "arbitrary"),
        name="rwkv7_scan",
    )(r3(r), r3(v), r3(nk), r4(kd), r4(bb), r4(cl), r4(lw))


def _rwkv_output(y, bonus, gate, lnx_g, lnx_b, ones_bd):
    inv_hd = 1.0 / RWKV_HD
    ym = _head_sum(y, ones_bd) * inv_hd
    yc = y - ym
    yv = _head_sum(yc * yc, ones_bd) * inv_hd
    y = yc * lax.rsqrt(yv + LNX_EPS) * lnx_g + lnx_b + bonus
    return (y * gate).astype(BF16)


def _diff_attn_kernel(qt_ref, k_ref, vt_ref, lam_ref, ng_ref, o_ref, *, lam_init):
    tq = qt_ref.shape[3]
    s = k_ref.shape[1]
    qt = qt_ref[0, 0]
    row = lax.broadcasted_iota(jnp.int32, qt.shape, 0)
    zero = jnp.zeros((), qt.dtype)
    wq = jnp.concatenate([jnp.where(row < DIFF_HD, qt, zero), jnp.where(row >= DIFF_HD, qt, zero)], axis=1)
    pad = 2 * V7X_SUBLANES
    ones_rows = (lax.broadcasted_iota(jnp.int32, (pad, ATT_K_TILE), 0) == 0).astype(BF16)

    ntiles = s // ATT_K_TILE
    scores = lambda i: _dot(k_ref[0, i * ATT_K_TILE:(i + 1) * ATT_K_TILE, :], wq)
    m_old = jnp.full((1, 2 * tq), -1e30, F32)
    acc = jnp.zeros((DIFF_VD + pad, 2 * tq), F32)
    sc = scores(0)
    for i in range(ntiles):
        sc_next = scores(i + 1) if i + 1 < ntiles else None
        m_new = jnp.maximum(m_old, jnp.max(sc, axis=0, keepdims=True))
        p = jnp.exp2(sc - m_new).astype(BF16)
        vt_ext = jnp.concatenate([vt_ref[0, 0, :, i * ATT_K_TILE:(i + 1) * ATT_K_TILE], ones_rows], axis=0)
        acc = jnp.exp2(m_old - m_new) * acc + _dot(vt_ext, p)
        m_old, sc = m_new, sc_next

    lam = lam_ref[...]
    lam_full = (jnp.exp(jnp.sum(lam[0:1] * lam[1:2], keepdims=True))
                - jnp.exp(jnp.sum(lam[2:3] * lam[3:4], keepdims=True)) + lam_init)
    o = acc[0:DIFF_VD, :] / acc[DIFF_VD:DIFF_VD + 1, :]
    o = o[:, 0:tq] - lam_full * o[:, tq:2 * tq]
    o = o * lax.rsqrt(jnp.mean(o * o, axis=0, keepdims=True) + RMS_EPS) * ng_ref[...] * (1.0 - lam_init)
    o_ref[0] = o.T.astype(BF16)


def _diff_attention(q_t, k3, v_t, lam, norm_g, *, lam_init):
    b, h, _, s = q_t.shape
    tq = min(ATT_Q_TILE, s)
    return pl.pallas_call(
        functools.partial(_diff_attn_kernel, lam_init=lam_init),
        grid=(b, h, s // tq),
        in_specs=[pl.BlockSpec((1, 1, 2 * DIFF_HD, tq), lambda bi, hi, qi: (bi, hi, 0, qi)),
                  pl.BlockSpec((1, s, 2 * DIFF_HD), lambda bi, hi, qi: (bi, 0, hi)),
                  pl.BlockSpec((1, 1, DIFF_VD, s), lambda bi, hi, qi: (bi, hi, 0, 0)),
                  _resident((4, DIFF_HD)), _resident((DIFF_VD, 1))],
        out_specs=pl.BlockSpec((1, tq, DIFF_VD), lambda bi, hi, qi: (bi, qi, hi)),
        out_shape=jax.ShapeDtypeStruct((b, s, h * DIFF_VD), BF16),
        compiler_params=_params("parallel", "parallel", "arbitrary"),
        name="diff_attention",
    )(q_t, k3, v_t, lam, norm_g)


def _fnet_stage1_kernel(y_ref, m_ref, o_ref):
    n1 = y_ref.shape[1]
    c = BRANCH_W
    pq = _dot(m_ref[...], y_ref[0])
    for t2 in range(FNET_COL_TILE):
        re = slice(2 * c * t2, 2 * c * t2 + c)
        im = slice(2 * c * t2 + c, 2 * c * (t2 + 1))
        o_ref[0, :, re] = (pq[0:n1, re] + pq[n1:2 * n1, im]).astype(BF16)
        o_ref[0, :, im] = (pq[0:n1, im] - pq[n1:2 * n1, re]).astype(BF16)


def _fnet_stage2_kernel(a_ref, e_ref, o_ref, *, scale):
    c = BRANCH_W
    n2 = FNET_N2
    for f in range(FNET_F1_TILE):
        res = _dot(e_ref[f, :, 0:n2], a_ref[0, f, :, 0:c]) + _dot(e_ref[f, :, n2:2 * n2], a_ref[0, f, :, c:2 * c])
        o_ref[0, :, f * c:(f + 1) * c] = (res * scale).astype(BF16)


def _fourier_mixer(yf3, consts):
    b, s, c2 = yf3.shape
    n2 = FNET_N2
    n1 = s // n2
    width = FNET_COL_TILE * c2
    stage1 = pl.pallas_call(
        _fnet_stage1_kernel,
        grid=(b, n2 * c2 // width),
        in_specs=[pl.BlockSpec((1, n1, width), lambda bi, j: (bi, 0, j)),
                  _resident((2 * n1, n1))],
        out_specs=pl.BlockSpec((1, n1, width), lambda bi, j: (bi, 0, j)),
        out_shape=jax.ShapeDtypeStruct((b, n1, n2 * c2), BF16),
        compiler_params=_params("parallel", "parallel"),
        name="fnet_stage1",
    )(yf3.reshape(b, n1, n2 * c2), consts["dft1"][s])
    a4 = stage1.reshape(b, n1, n2, c2)
    f1t = FNET_F1_TILE
    out = pl.pallas_call(
        functools.partial(_fnet_stage2_kernel, scale=float(1.0 / math.sqrt(s * FNET_GD))),
        grid=(b, n1 // f1t),
        in_specs=[pl.BlockSpec((1, f1t, n2, c2), lambda bi, j: (bi, j, 0, 0)),
                  pl.BlockSpec((f1t, n2, 2 * n2), lambda bi, j: (j, 0, 0))],
        out_specs=pl.BlockSpec((1, n2, f1t * BRANCH_W), lambda bi, j: (bi, 0, j)),
        out_shape=jax.ShapeDtypeStruct((b, n2, n1 * BRANCH_W), BF16),
        compiler_params=_params("parallel", "parallel"),
        name="fnet_stage2",
    )(a4, consts["dft2"][s])
    return out.reshape(b, s, BRANCH_W)


def _pool_branch(p, prev8, next8, first, last, pos0, seq, scale, ext_ref):
    tm = p.shape[0]
    halo = V7X_SUBLANES
    ext_ref[0:halo, :] = prev8 * (1.0 - first.astype(F32))
    ext_ref[halo:halo + tm, :] = p
    ext_ref[halo + tm:2 * halo + tm, :] = next8 * (1.0 - last.astype(F32))
    pos = pos0 + lax.broadcasted_iota(jnp.int32, (tm, 1), 0)
    outs = []
    for g, win in enumerate(POOL_WINDOWS):
        hw = win // 2
        cols = slice(g * POOL_GD, (g + 1) * POOL_GD)
        acc = ext_ref[halo - hw:halo - hw + tm, cols]
        for off in range(-hw + 1, hw):
            acc = acc + ext_ref[halo + off:halo + off + tm, cols]
        count = (jnp.minimum(pos + hw, seq) - jnp.maximum(pos - hw, 0)).astype(F32)
        outs.append(acc / count - p[:, cols])
    return jnp.concatenate(outs, axis=1) * scale


def _merge_kernel(x_ref, mod_ref, sf_ref, sb_ref, bonus_ref, rgate_ref, yd_ref, yf_ref, pp_ref, ppp_ref, ppn_ref,
                  gate_ref, wb_ref, wo_ref, ps_ref, g_ref, b_ref, lxg_ref, lxb_ref, ones_ref, o_ref, ext_ref,
                  *, tps, seq, alpha):
    i = pl.program_id(0)
    tm = x_ref.shape[0]
    it = i % tps
    y_pool = _pool_branch(pp_ref[...], ppp_ref[...], ppn_ref[...], it == 0, it == tps - 1,
                          it * tm, seq, ps_ref[...], ext_ref)
    y_rwkv = _rwkv_output(sf_ref[0] + sb_ref[0], bonus_ref[...], rgate_ref[...], lxg_ref[...], lxb_ref[...],
                          ones_ref[...])
    branches = (y_rwkv, yd_ref[...], yf_ref[...], y_pool.astype(BF16))
    merged = jnp.zeros((tm, D_MODEL), F32)
    for n, br in enumerate(branches):
        gate = gate_ref[:, n * D_MODEL:(n + 1) * D_MODEL].astype(F32)
        merged = merged + gate * _dot(br, wb_ref[n])
    upd = _dot(merged.astype(BF16), wo_ref[...])
    mod = mod_ref[0]
    y = alpha * x_ref[...] + (1.0 + mod[5:6]) * upd
    o_ref[...] = _layer_norm(y) * g_ref[...] + b_ref[...]


def _merge_sublayer(x2, mod, y_dirs, bonus, rgate, y_diff, y_fnet, p_pool, gates, w_branch, w_out, pool_scale,
                    ln_g, ln_b, lnx_g, lnx_b, ones_bd, *, seq, alpha):
    t = x2.shape[0]
    tm = min(TOKEN_TILE, seq)
    tps = seq // tm
    r8 = tm // V7X_SUBLANES
    nblk8 = t // V7X_SUBLANES
    row = lambda i: (i, 0)
    br = lambda: pl.BlockSpec((tm, BRANCH_W), row)
    return pl.pallas_call(
        functools.partial(_merge_kernel, tps=tps, seq=seq, alpha=alpha),
        grid=(t // tm,),
        in_specs=[pl.BlockSpec((tm, D_MODEL), row),
                  pl.BlockSpec((1, 9, D_MODEL), lambda i: (i // tps, 0, 0)),
                  pl.BlockSpec((1, tm, BRANCH_W), lambda i: (0, i, 0)),
                  pl.BlockSpec((1, tm, BRANCH_W), lambda i: (1, i, 0)),
                  br(), br(), br(), br(), br(),
                  pl.BlockSpec((V7X_SUBLANES, BRANCH_W), lambda i: (jnp.maximum(i * r8 - 1, 0), 0)),
                  pl.BlockSpec((V7X_SUBLANES, BRANCH_W), lambda i: (jnp.minimum((i + 1) * r8, nblk8 - 1), 0)),
                  pl.BlockSpec((tm, N_BRANCH * D_MODEL), row),
                  _resident((N_BRANCH, BRANCH_W, D_MODEL)), _resident((D_MODEL, D_MODEL)),
                  _resident((1, BRANCH_W)), _resident((1, D_MODEL)), _resident((1, D_MODEL)),
                  _resident((1, BRANCH_W)), _resident((1, BRANCH_W)), _resident((BRANCH_W, BRANCH_W))],
        out_specs=pl.BlockSpec((tm, D_MODEL), row),
        out_shape=jax.ShapeDtypeStruct((t, D_MODEL), F32),
        scratch_shapes=[pltpu.VMEM((tm + 2 * V7X_SUBLANES, BRANCH_W), F32)],
        compiler_params=_params("parallel"),
        name="merge_sublayer",
    )(x2, mod, y_dirs, y_dirs, bonus, rgate, y_diff, y_fnet, p_pool, p_pool, p_pool, gates, w_branch, w_out,
      pool_scale, ln_g, ln_b, lnx_g, lnx_b, ones_bd)


def _rope_tables(s):
    inv = ROPE_THETA ** (-jnp.arange(0, ROPE_DIM, 2, dtype=F32) / ROPE_DIM)
    ang = jnp.arange(s, dtype=F32)[:, None] * inv[None, :]
    cos, sin = jnp.cos(ang), jnp.sin(ang)
    half = ROPE_DIM // 2
    pad = jnp.zeros((s, DIFF_HD - ROPE_DIM), F32)
    zero = jnp.zeros((s, half), F32)
    c64 = jnp.concatenate([cos, cos, pad + 1.0], axis=1)
    s1_64 = jnp.concatenate([-sin, zero, pad], axis=1)
    s2_64 = jnp.concatenate([zero, sin, pad], axis=1)
    return tuple(jnp.concatenate([x, x], axis=1) for x in (c64, s1_64, s2_64))


def _group_dft_matrix():
    c = np.arange(FNET_GD)
    ang = 2.0 * np.pi * ((c[:, None] * c[None, :]) % FNET_GD) / FNET_GD
    eye = np.eye(FNET_GROUPS)
    return jnp.asarray(np.concatenate([np.kron(eye, np.cos(ang)), -np.kron(eye, np.sin(ang))], axis=1), BF16)


def _seq_dft_tables(s):
    n2 = FNET_N2
    n1 = s // n2
    a = np.arange(n1)
    ang1 = 2.0 * np.pi * ((a[:, None] * a[None, :]) % n1) / n1
    dft1 = np.concatenate([np.cos(ang1), np.sin(ang1)], axis=0)
    f = a[:, None, None] + n1 * np.arange(n2)[None, :, None]
    ang2 = 2.0 * np.pi * ((f * np.arange(n2)[None, None, :]) % s) / s
    dft2 = np.concatenate([np.cos(ang2), np.sin(ang2)], axis=2)
    return jnp.asarray(dft1, BF16), jnp.asarray(dft2, BF16)


def _scan_cumsum_matrices(ts):
    t = np.arange(ts)
    same = (t[:, None] // SCAN_CHUNK) == (t[None, :] // SCAN_CHUNK)
    fwd = same & (t[None, :] <= t[:, None])
    bwd = same & (t[None, :] >= t[:, None])
    return jnp.asarray(np.stack([fwd, bwd]).astype(np.float32), BF16)


def _constants(seqs):
    h = np.arange(BRANCH_W) // RWKV_HD
    consts = {
        "ones_bd": jnp.asarray((h[:, None] == h[None, :]).astype(np.float32), BF16),
        "fmat": _group_dft_matrix(),
        "rope": {}, "dft1": {}, "dft2": {}, "tri": {},
    }
    for s in seqs:
        consts["rope"][s] = _rope_tables(s)
        consts["dft1"][s], consts["dft2"][s] = _seq_dft_tables(s)
        ts = min(TOKEN_TILE, s)
        consts["tri"][ts] = _scan_cumsum_matrices(ts)
    return consts


def _pad_lora(w2):
    z = jnp.zeros_like(w2[0])
    return jnp.stack([jnp.concatenate([w2[0], z], axis=0), jnp.concatenate([z, w2[1]], axis=0)]).astype(BF16)


def _encoder_layer(x2, b, s, mod, lam_init, w, consts):
    alpha = (2.0 * DEPTH) ** 0.25
    x2 = _ffn_sublayer(x2, mod, w["ln_g"][0], w["ln_b"][0], w["ffa_in"], w["ffa_out"], sub=0, seq=s, alpha=alpha)
    p_rwkv, q_t, k_tok, v_t, p_fnet, p_pool, gates = _mixer_projection(x2, mod, w["w_in"], consts["rope"][s],
                                                               consts["fmat"], seq=s)
    prep = _rwkv_prep(p_rwkv, w["rwkv"], consts, seq=s)
    y_dirs = _rwkv_scan(prep[:7], b, s)
    y_diff = _diff_attention(q_t, k_tok.reshape(b, s, BRANCH_W), v_t, w["diff_lam"], w["diff_norm_g"],
                             lam_init=lam_init)
    y_fnet = _fourier_mixer(p_fnet.reshape(b, s, 2 * BRANCH_W), consts)
    x2 = _merge_sublayer(x2, mod, y_dirs.reshape(2, b * s, BRANCH_W), prep[7], prep[8],
                         y_diff.reshape(b * s, BRANCH_W), y_fnet.reshape(b * s, BRANCH_W), p_pool, gates,
                         w["w_branch"], w["w_out"], w["pool_scale"], w["ln_g"][1], w["ln_b"][1],
                         w["rwkv"]["lnx_g"], w["rwkv"]["lnx_b"], consts["ones_bd"], seq=s, alpha=alpha)
    x2 = _ffn_sublayer(x2, mod, w["ln_g"][2], w["ln_b"][2], w["ffb_in"], w["ffb_out"], sub=2, seq=s, alpha=alpha)
    return x2


def kernel(x_prompt, x_sample, c_prompt, c_sample, ada_w, ada_b, ln_g, ln_b, ffa_w_in, ffa_w_out, ffb_w_in, ffb_w_out, w_in, rwkv_mu, rwkv_w0, rwkv_w2, rwkv_a0, rwkv_a2, rwkv_g2, rwkv_kk, rwkv_ka, rwkv_rk, rwkv_lnx_g, rwkv_lnx_b, diff_lam, diff_norm_g, pool_scale, w_branch, w_out):
    groups = ((x_prompt, c_prompt), (x_sample, c_sample))
    consts = _constants(sorted({x.shape[1] for x, _ in groups}))
    depth = ada_w.shape[0]

    nb = [c.shape[0] for _, c in groups]
    c_all = jnp.concatenate([c for _, c in groups], axis=0)
    rows = -(-c_all.shape[0] // V7X_SUBLANES) * V7X_SUBLANES
    c_pad = jnp.pad(c_all, ((0, rows - c_all.shape[0]), (0, 0)))

    layers = []
    for l in range(depth):
        mod_all = _modulation(c_pad, ada_w[l], ada_b[l]).reshape(rows, 9, D_MODEL)
        vec = lambda a: a[l].reshape(1, -1)
        layers.append({
            "mods": (mod_all[:nb[0]], mod_all[nb[0]:nb[0] + nb[1]]),
            "lam_init": 0.8 - 0.6 * math.exp(-0.3 * l),
            "ln_g": [ln_g[l, j].reshape(1, -1) for j in range(3)],
            "ln_b": [ln_b[l, j].reshape(1, -1) for j in range(3)],
            "ffa_in": ffa_w_in[l].astype(BF16), "ffa_out": ffa_w_out[l].astype(BF16),
            "ffb_in": ffb_w_in[l].astype(BF16), "ffb_out": ffb_w_out[l].astype(BF16),
            "w_in": w_in[l].astype(BF16),
            "w_branch": w_branch[l].astype(BF16), "w_out": w_out[l].astype(BF16),
            "pool_scale": vec(pool_scale),
            "diff_lam": diff_lam[l], "diff_norm_g": diff_norm_g[l].reshape(-1, 1),
            "rwkv": {
                "mu": vec(rwkv_mu), "kk": vec(rwkv_kk), "ka": vec(rwkv_ka), "rk": vec(rwkv_rk),
                "lnx_g": vec(rwkv_lnx_g), "lnx_b": vec(rwkv_lnx_b),
                "w0": rwkv_w0[l].reshape(2, 1, BRANCH_W), "a0": rwkv_a0[l].reshape(2, 1, BRANCH_W),
                "w2pad": _pad_lora(rwkv_w2[l]), "a2pad": _pad_lora(rwkv_a2[l]),
                "g2": rwkv_g2[l].astype(BF16),
            },
        })

    outs = []
    for gi, (x, _) in enumerate(groups):
        b, s, d = x.shape
        x2 = x.reshape(b * s, d)
        for w in layers:
            x2 = _encoder_layer(x2, b, s, w["mods"][gi], w["lam_init"], w, consts)
        outs.append(x2.reshape(b, s, d))
    return tuple(outs)
```
